```python
import jax, jax.numpy as jnp
from jax import lax
import numpy as np

D_MODEL = 1024
BATCH = 16
SEQ = 4096
DEPTH = 1

CHUNK = 64
MIX_WIDTH = D_MODEL
M_HEADS = 4
M_HEAD_DIM = 128
M_WIDTH = M_HEADS * M_HEAD_DIM
CONV_WIDTH = 4
A_HEADS = 4
A_HEAD_DIM = 128
A_WIDTH = A_HEADS * A_HEAD_DIM
IDX_HEADS = 4
IDX_DIM = 64
TOPK_KEYS_MAX = 256
Q_BLOCK = 128
ROPE_THETA = 500000.0
ROPE_FRACTION_DIV = 4
N_EXPERTS = 64
N_GROUPS = 8
TOPK_GROUPS = 4
TOPK_EXPERTS = 8
D_EXPERT = 256
D_SHARED = 256
ROUTE_SCALE = 2.5
EXPERT_BLOCK = 512
NORM_EPS = 1e-6
STAB_INIT = -1e30
SPLIT_SIZES = (M_WIDTH, M_WIDTH, M_WIDTH, M_WIDTH, M_HEADS, M_HEADS, A_WIDTH, A_HEAD_DIM, A_HEAD_DIM, IDX_HEADS * IDX_DIM, IDX_DIM, IDX_HEADS)
IN_DIM = 4 * M_WIDTH + 2 * M_HEADS + A_WIDTH + 2 * A_HEAD_DIM + IDX_HEADS * IDX_DIM + IDX_DIM + IDX_HEADS

kernel_name = "hymba_mlstm_dsa_moe_block"


def rms_norm(x, g):
    xf = x.astype(jnp.float32)
    y = xf * lax.rsqrt(jnp.mean(xf * xf, axis=-1, keepdims=True) + NORM_EPS)
    return (y * g.astype(jnp.float32)).astype(x.dtype)


def layer_norm(x, g, b):
    xf = x.astype(jnp.float32)
    mu = jnp.mean(xf, axis=-1, keepdims=True)
    var = jnp.mean(jnp.square(xf - mu), axis=-1, keepdims=True)
    y = (xf - mu) * lax.rsqrt(var + NORM_EPS)
    return (y * g.astype(jnp.float32) + b.astype(jnp.float32)).astype(x.dtype)


def head_rms_norm(h, g):
    y = h * lax.rsqrt(jnp.mean(h * h, axis=-1, keepdims=True) + NORM_EPS)
    return y * g.astype(jnp.float32).reshape(h.shape[-2:])


def partial_rope(x, pos):
    dh = x.shape[-1]
    rot = dh // ROPE_FRACTION_DIV
    half = rot // 2
    inv_freq = jnp.exp(-jnp.log(jnp.float32(ROPE_THETA)) * jnp.arange(half, dtype=jnp.float32) * (2.0 / rot))
    ang = pos.astype(jnp.float32)[:, None] * inv_freq[None, :]
    shape = (1, pos.shape[0]) + (1,) * (x.ndim - 3) + (half,)
    cos = jnp.cos(ang).reshape(shape)
    sin = jnp.sin(ang).reshape(shape)
    xf = x.astype(jnp.float32)
    x1 = xf[..., :half]
    x2 = xf[..., half:rot]
    out = jnp.concatenate([x1 * cos - x2 * sin, x2 * cos + x1 * sin, xf[..., rot:]], axis=-1)
    return out.astype(x.dtype)


def causal_conv(u, w, b):
    ch = u.shape[-1]
    out = lax.conv_general_dilated(u, w.astype(u.dtype)[:, None, :], window_strides=(1,), padding=[(CONV_WIDTH - 1, 0)], dimension_numbers=("NWC", "WIO", "NWC"), feature_group_count=ch)
    return out + b.astype(u.dtype)


def mlstm_chunkwise(q, k, v, i_pre, f_pre):
    B, S, H, Dh = q.shape
    NC = S // CHUNK
    f32 = jnp.float32

    def chunks(a):
        a = a.astype(f32).reshape((B, NC, CHUNK) + a.shape[2:])
        return jnp.moveaxis(jnp.moveaxis(a, 1, 0), 3, 2)

    qc = chunks(q)
    kc = chunks(k) * (Dh ** -0.5)
    vc = chunks(v)
    ic = chunks(i_pre)
    lfc = jax.nn.log_sigmoid(chunks(f_pre))
    tril = jnp.tril(jnp.ones((CHUNK, CHUNK), dtype=bool))

    def step(carry, inp):
        C, n, m = carry
        qb, kb, vb, ib, lfb = inp
        b = jnp.cumsum(lfb, axis=-1)
        logD = b[..., :, None] - b[..., None, :] + ib[..., None, :]
        logD = jnp.where(tril, logD, -jnp.inf)
        m_inter = b + m[..., None]
        m_t = jnp.maximum(m_inter, jnp.max(logD, axis=-1))
        dmat = jnp.exp(logD - m_t[..., None])
        s_qk = jnp.einsum("bhtd,bhsd->bhts", qb, kb) * dmat
        inter = jnp.exp(m_inter - m_t)
        num = jnp.einsum("bhts,bhsd->bhtd", s_qk, vb) + inter[..., None] * jnp.einsum("bhvk,bhtk->bhtv", C, qb)
        den = jnp.sum(s_qk, axis=-1) + inter * jnp.einsum("bhk,bhtk->bht", n, qb)
        h = num / jnp.maximum(jnp.abs(den), jnp.exp(-m_t))[..., None]
        bL = b[..., -1]
        log_w = bL[..., None] - b + ib
        m_new = jnp.maximum(bL + m, jnp.max(log_w, axis=-1))
        w = jnp.exp(log_w - m_new[..., None])
        decay = jnp.exp(bL + m - m_new)
        C_new = decay[..., None, None] * C + jnp.einsum("bhsv,bhsk->bhvk", vb * w[..., None], kb)
        n_new = decay[..., None] * n + jnp.einsum("bhs,bhsk->bhk", w, kb)
        return (C_new, n_new, m_new), h

    init = (jnp.zeros((B, H, Dh, Dh), f32), jnp.zeros((B, H, Dh), f32), jnp.full((B, H), STAB_INIT, f32))
    _, hs = lax.scan(step, init, (qc, kc, vc, ic, lfc))
    return hs.transpose(1, 0, 3, 2, 4).reshape(B, S, H, Dh)


def dsa_attention(aq, ak, av, iq, ik, iw):
    B, S, H, Dh = aq.shape
    k_top = min(TOPK_KEYS_MAX, S // 4)
    nqb = S // Q_BLOCK
    key_chunk = jnp.arange(S) // CHUNK
    scale = Dh ** -0.5

    def block(args):
        qb, iqb, iwb, qpos = args
        q_chunk = qpos // CHUNK
        rel = jax.nn.relu(jnp.einsum("bqhd,bsd->bqhs", iqb, ik))
        score = jnp.einsum("bqh,bqhs->bqs", iwb, rel).astype(jnp.float32)
        admissible = key_chunk[None, :] <= q_chunk[:, None]
        score = jnp.where(admissible[None], score, -jnp.inf)
        _, idx = lax.top_k(score, k_top)
        valid = (idx // CHUNK) <= q_chunk[None, :, None]
        kg = jax.vmap(lambda kk, ii: kk[ii])(ak, idx)
        vg = jax.vmap(lambda vv, ii: vv[ii])(av, idx)
        logits = jnp.einsum("bqhd,bqkd->bhqk", qb, kg).astype(jnp.float32) * scale
        logits = jnp.where(valid[:, None], logits, -jnp.inf)
        p = jax.nn.softmax(logits, axis=-1).astype(vg.dtype)
        return jnp.einsum("bhqk,bqkd->bqhd", p, vg)

    def qblocks(a):
        return jnp.moveaxis(a.reshape((B, nqb, Q_BLOCK) + a.shape[2:]), 1, 0)

    out = lax.map(block, (qblocks(aq), qblocks(iq), qblocks(iw), jnp.arange(S).reshape(nqb, Q_BLOCK)))
    return jnp.moveaxis(out, 0, 1).reshape(B, S, H, Dh)


def moe_ffn(h, w_router, b_router, w_gate, w_up, w_down, ws_gate, ws_up, ws_down):
    B, S, D = h.shape
    T = B * S
    hf = h.reshape(T, D)
    scores = jax.nn.sigmoid(hf.astype(jnp.float32) @ w_router.astype(jnp.float32))
    sel = scores + b_router.astype(jnp.float32)
    grp = sel.reshape(T, N_GROUPS, N_EXPERTS // N_GROUPS)
    grp_score = jnp.sum(lax.top_k(grp, 2)[0], axis=-1)
    _, top_g = lax.top_k(grp_score, TOPK_GROUPS)
    gmask = jnp.any(top_g[..., None] == jnp.arange(N_GROUPS), axis=-2)
    emask = jnp.repeat(gmask, N_EXPERTS // N_GROUPS, axis=-1)
    _, eidx = lax.top_k(jnp.where(emask, sel, -jnp.inf), TOPK_EXPERTS)
    gw = jnp.take_along_axis(scores, eidx, axis=-1)
    gw = gw / jnp.sum(gw, axis=-1, keepdims=True) * ROUTE_SCALE

    A = T * TOPK_EXPERTS
    e_flat = eidx.reshape(A)
    tok_flat = jnp.repeat(jnp.arange(T, dtype=jnp.int32), TOPK_EXPERTS)
    g_flat = gw.reshape(A)
    order = jnp.argsort(e_flat)
    se, st, sg = e_flat[order], tok_flat[order], g_flat[order]
    counts = jax.ops.segment_sum(jnp.ones_like(e_flat), e_flat, num_segments=N_EXPERTS)
    padded = (counts + EXPERT_BLOCK - 1) // EXPERT_BLOCK * EXPERT_BLOCK
    pad_end = jnp.cumsum(padded)
    pad_start = pad_end - padded
    start = jnp.cumsum(counts) - counts
    dest = pad_start[se] + jnp.arange(A, dtype=se.dtype) - start[se]
    nb = (A + EXPERT_BLOCK - 1) // EXPERT_BLOCK + N_EXPERTS
    rows = nb * EXPERT_BLOCK
    row_tok = jnp.zeros((rows,), jnp.int32).at[dest].set(st)
    row_g = jnp.zeros((rows,), hf.dtype).at[dest].set(sg.astype(hf.dtype))
    block_e = jnp.clip(jnp.searchsorted(pad_end, jnp.arange(nb, dtype=pad_end.dtype) * EXPERT_BLOCK, side="right"), 0, N_EXPERTS - 1)

    def body(y, args):
        tok_b, g_b, e_b = args
        xb = hf[tok_b]
        u = jax.nn.silu(xb @ w_gate[e_b]) * (xb @ w_up[e_b])
        yb = (u @ w_down[e_b]) * g_b[:, None]
        return y.at[tok_b].add(yb), None

    routed, _ = lax.scan(body, jnp.zeros((T, D), hf.dtype), (row_tok.reshape(nb, EXPERT_BLOCK), row_g.reshape(nb, EXPERT_BLOCK), block_e))
    shared = (jax.nn.silu(hf @ ws_gate) * (hf @ ws_up)) @ ws_down
    return (routed + shared).reshape(B, S, D)


def setup_inputs(seed: int = 0) -> dict:
    key = jax.random.key(seed)
    ks = jax.random.split(key, 32)
    L, D = DEPTH, D_MODEL

    def nrm(k, shape, scale):
        return jax.random.normal(k, shape, jnp.float32) * scale

    def gain(k, shape):
        return 1.0 + 0.02 * jax.random.normal(k, shape, jnp.float32)

    return {
        "x": nrm(ks[0], (BATCH, SEQ, D), 1.0),
        "c": nrm(ks[1], (BATCH, D), 1.0),
        "w_ada": nrm(ks[2], (L, D, 6 * D), 0.3 * D ** -0.5),
        "b_ada": nrm(ks[3], (L, 6 * D), 0.02),
        "g_mix_pre": gain(ks[4], (L, D)),
        "g_mix_post": gain(ks[5], (L, D)),
        "g_ffn_pre": gain(ks[6], (L, D)),
        "g_ffn_post": gain(ks[7], (L, D)),
        "w_in": nrm(ks[8], (L, D, IN_DIM), D ** -0.5),
        "conv_w": nrm(ks[9], (L, CONV_WIDTH, 2 * M_WIDTH), CONV_WIDTH ** -0.5),
        "conv_b": nrm(ks[10], (L, 2 * M_WIDTH), 0.02),
        "b_igate": nrm(ks[11], (L, M_HEADS), 0.1),
        "b_fgate": jnp.linspace(3.0, 6.0, M_HEADS, dtype=jnp.float32)[None, :] + nrm(ks[12], (L, M_HEADS), 0.02),
        "g_mlstm_head": gain(ks[13], (L, M_WIDTH)),
        "g_idx_k": gain(ks[14], (L, IDX_DIM)),
        "b_idx_k": nrm(ks[15], (L, IDX_DIM), 0.02),
        "w_out": nrm(ks[16], (L, MIX_WIDTH, D), MIX_WIDTH ** -0.5),
        "w_router": nrm(ks[17], (L, D, N_EXPERTS), D ** -0.5),
        "b_router": nrm(ks[18], (L, N_EXPERTS), 0.01),
        "w_exp_gate": nrm(ks[19], (L, N_EXPERTS, D, D_EXPERT), D ** -0.5),
        "w_exp_up": nrm(ks[20], (L, N_EXPERTS, D, D_EXPERT), D ** -0.5),
        "w_exp_down": nrm(ks[21], (L, N_EXPERTS, D_EXPERT, D), D_EXPERT ** -0.5),
        "w_sh_gate": nrm(ks[22], (L, D, D_SHARED), D ** -0.5),
        "w_sh_up": nrm(ks[23], (L, D, D_SHARED), D ** -0.5),
        "w_sh_down": nrm(ks[24], (L, D_SHARED, D), D_SHARED ** -0.5),
    }


def reference(x, c, w_ada, b_ada, g_mix_pre, g_mix_post, g_ffn_pre, g_ffn_post, w_in, conv_w, conv_b, b_igate, b_fgate, g_mlstm_head, g_idx_k, b_idx_k, w_out, w_router, b_router, w_exp_gate, w_exp_up, w_exp_down, w_sh_gate, w_sh_up, w_sh_down):
    B, S, D = x.shape
    pos = jnp.arange(S)
    offsets = np.cumsum(SPLIT_SIZES)[:-1].tolist()
    idx_w_scale = (IDX_HEADS ** -0.5) * (IDX_DIM ** -0.5)
    for l in range(DEPTH):
        mod = jax.nn.silu(c) @ w_ada[l] + b_ada[l]
        sh1, sc1, gt1, sh2, sc2, gt2 = jnp.split(mod[:, None, :], 6, axis=-1)

        h = rms_norm(x, g_mix_pre[l]) * (1.0 + sc1) + sh1
        proj = h @ w_in[l]
        mq, mk, mv, mo, mi, mf, aq, ak, av, iq, ik, iw = jnp.split(proj, offsets, axis=-1)

        qk = jax.nn.silu(causal_conv(jnp.concatenate([mq, mk], axis=-1), conv_w[l], conv_b[l]))
        mq, mk = jnp.split(qk, 2, axis=-1)
        hm = mlstm_chunkwise(mq.reshape(B, S, M_HEADS, M_HEAD_DIM), mk.reshape(B, S, M_HEADS, M_HEAD_DIM), mv.reshape(B, S, M_HEADS, M_HEAD_DIM), mi + b_igate[l], mf + b_fgate[l])
        hm = head_rms_norm(hm, g_mlstm_head[l]).reshape(B, S, M_WIDTH)
        hm = (jax.nn.sigmoid(mo.astype(jnp.float32)) * hm).astype(x.dtype)

        aq = partial_rope(aq.reshape(B, S, A_HEADS, A_HEAD_DIM), pos)
        ak = partial_rope(ak, pos)
        iq = partial_rope(iq.reshape(B, S, IDX_HEADS, IDX_DIM), pos)
        ik = partial_rope(layer_norm(ik, g_idx_k[l], b_idx_k[l]), pos)
        ha = dsa_attention(aq, ak, av, iq, ik, iw * idx_w_scale).reshape(B, S, A_WIDTH)

        y = jnp.concatenate([hm, ha], axis=-1) @ w_out[l]
        x = x + gt1 * rms_norm(y, g_mix_post[l])

        h2 = rms_norm(x, g_ffn_pre[l]) * (1.0 + sc2) + sh2
        y2 = moe_ffn(h2, w_router[l], b_router[l], w_exp_gate[l], w_exp_up[l], w_exp_down[l], w_sh_gate[l], w_sh_up[l], w_sh_down[l])
        x = x + gt2 * rms_norm(y2, g_ffn_post[l])
    return x
```

```python
import functools

import jax
import jax.numpy as jnp
import numpy as np
from jax import lax
from jax.experimental import pallas as pl
from jax.experimental.pallas import tpu as pltpu

F32 = jnp.float32
BF16 = jnp.bfloat16
I32 = jnp.int32
U32 = jnp.uint32

CHUNK = 64
M_HEADS = 4
M_HEAD_DIM = 128
M_WIDTH = M_HEADS * M_HEAD_DIM
CONV_WIDTH = 4
A_HEADS = 4
A_HEAD_DIM = 128
A_WIDTH = A_HEADS * A_HEAD_DIM
IDX_HEADS = 4
IDX_DIM = 64
TOPK_KEYS_MAX = 256
Q_BLOCK = 128
ROPE_THETA = 500000.0
ROPE_FRACTION_DIV = 4
N_EXPERTS = 64
N_GROUPS = 8
GROUP_SIZE = N_EXPERTS // N_GROUPS
TOPK_GROUPS = 4
TOPK_EXPERTS = 8
D_EXPERT = 256
ROUTE_SCALE = 2.5
NORM_EPS = 1e-6
STAB_INIT = -1e30

LANES = 128
SUBLANES = 8
VMEM_LIMIT = 56 * 1024 * 1024

ROW_TILE = 512
MLSTM_CHUNK = 64
KEY_TILE = 512
ROUTE_TILE = 512
EXPERT_ROWS = 512
DISPATCH_TILE = 256
COMBINE_TILE = 256

C_MQ = 0
C_MV = 2 * M_WIDTH
C_MO = 3 * M_WIDTH
C_AQ = 4 * M_WIDTH
C_AK = C_AQ + A_WIDTH
C_AV = C_AK + A_HEAD_DIM
C_IQ = C_AV + A_HEAD_DIM
C_SM = C_IQ + IDX_HEADS * IDX_DIM
IN_PAD = C_SM + LANES
SM_MI = IDX_DIM
SM_MF = IDX_DIM + M_HEADS
SM_IW = IDX_DIM + 2 * M_HEADS


def _params(sem, **kw):
    return pltpu.CompilerParams(dimension_semantics=sem, vmem_limit_bytes=VMEM_LIMIT, **kw)


def _split_bf16(x):
    hi = x.astype(BF16)
    lo = (x - hi.astype(F32)).astype(BF16)
    return hi, lo


def _dot(a, b):
    return jnp.dot(a, b, preferred_element_type=F32)


def _dot_nt(a, b):
    return lax.dot_general(a, b, (((1,), (1,)), ((), ())), preferred_element_type=F32)


def _dot3(a, b):
    ah, al = _split_bf16(a)
    bh, bl = _split_bf16(b)
    return _dot(ah, bh) + (_dot(ah, bl) + _dot(al, bh))


def _sigmoid(x):
    return 1.0 / (1.0 + jnp.exp(-x))


def _silu(x):
    return x * _sigmoid(x)


def _log_sigmoid(x):
    return jnp.minimum(x, 0.0) - jnp.log(1.0 + jnp.exp(-jnp.abs(x)))


def _adaln_kernel(c_ref, w_ref, b_ref, o_ref):
    o_ref[...] = _dot3(_silu(c_ref[...]), w_ref[...]) + b_ref[...]


def _adaln(c, w, b):
    bsz, d = c.shape
    n = w.shape[1]
    tn = d
    return pl.pallas_call(
        _adaln_kernel,
        out_shape=jax.ShapeDtypeStruct((bsz, n), F32),
        grid=(n // tn,),
        in_specs=[
            pl.BlockSpec((bsz, d), lambda j: (0, 0)),
            pl.BlockSpec((d, tn), lambda j: (0, j)),
            pl.BlockSpec((1, tn), lambda j: (0, j)),
        ],
        out_specs=pl.BlockSpec((bsz, tn), lambda j: (0, j)),
        compiler_params=_params(("arbitrary",)),
        name="adaln",
    )(c, w, b.reshape(1, n))


def _rope_tables(seq, head_dim, group):
    rot = head_dim // ROPE_FRACTION_DIV
    half = rot // 2
    inv_freq = jnp.exp(-jnp.log(jnp.float32(ROPE_THETA)) * jnp.arange(half, dtype=F32) * (2.0 / rot))
    ang = jnp.arange(seq, dtype=F32)[:, None] * inv_freq[None, :]
    cos, sin = jnp.cos(ang), jnp.sin(ang)
    pad = group - rot
    c = jnp.concatenate([cos, cos, jnp.ones((seq, pad), F32)], axis=1)
    s1 = jnp.concatenate([-sin, jnp.zeros((seq, half + pad), F32)], axis=1)
    s2 = jnp.concatenate([jnp.zeros((seq, half), F32), sin, jnp.zeros((seq, pad), F32)], axis=1)
    rep = LANES // group
    return jnp.tile(c, (1, rep)), jnp.tile(s1, (1, rep)), jnp.tile(s2, (1, rep)), half


def _rope128(x, c, s1, s2, half):
    back = pltpu.roll(x, LANES - half, 1)
    fwd = pltpu.roll(x, half, 1)
    return x * c + back * s1 + fwd * s2


def _inproj_kernel(x_ref, mod_ref, g_ref, w_ref, cw_ref, cb_ref, lng_ref, lnb_ref, sms_ref, smb_ref,
                   ac_ref, as1_ref, as2_ref, ic_ref, is1_ref, is2_ref,
                   qm_ref, km_ref, vm_ref, om_ref, aq_ref, ak_ref, av_ref, iq_ref, ik_ref, sm_ref,
                   ubuf, *, a_half, i_half):
    ts = x_ref.shape[1]
    x = x_ref[0]
    shift = mod_ref[0, 0:1, :]
    scale = mod_ref[0, 1:2, :]
    y = x * lax.rsqrt(jnp.mean(x * x, axis=-1, keepdims=True) + NORM_EPS)
    h = (y * g_ref[...]) * (1.0 + scale) + shift
    hb = h.astype(BF16)

    def proj(lo, width):
        return _dot(hb, w_ref[:, lo:lo + width])

    @pl.when(pl.program_id(1) == 0)
    def _():
        ubuf[0:SUBLANES, :] = jnp.zeros((SUBLANES, 2 * M_WIDTH), F32)

    ubuf[SUBLANES:SUBLANES + ts, :] = proj(C_MQ, 2 * M_WIDTH)
    acc = cb_ref[...] + cw_ref[0:1, :] * ubuf[SUBLANES - 3:SUBLANES - 3 + ts, :]
    for j in range(1, CONV_WIDTH):
        off = SUBLANES - (CONV_WIDTH - 1) + j
        acc = acc + cw_ref[j:j + 1, :] * ubuf[off:off + ts, :]
    ubuf[0:SUBLANES, :] = ubuf[ts:ts + SUBLANES, :]
    qk = _silu(acc)
    qm_ref[0] = qk[:, :M_WIDTH].astype(BF16)
    km_ref[0] = (qk[:, M_WIDTH:] * (M_HEAD_DIM ** -0.5)).astype(BF16)

    vm_ref[0] = proj(C_MV, M_WIDTH).astype(BF16)
    om_ref[0] = proj(C_MO, M_WIDTH)

    ac, as1, as2 = ac_ref[...], as1_ref[...], as2_ref[...]
    aq = proj(C_AQ, A_WIDTH)
    for hd in range(A_HEADS):
        sl = slice(hd * A_HEAD_DIM, (hd + 1) * A_HEAD_DIM)
        aq_ref[0, :, sl] = _rope128(aq[:, sl], ac, as1, as2, a_half).astype(BF16)
    ak_ref[0] = _rope128(proj(C_AK, A_HEAD_DIM), ac, as1, as2, a_half).astype(BF16)
    av_ref[0] = proj(C_AV, A_HEAD_DIM).astype(BF16)

    ic, is1, is2 = ic_ref[...], is1_ref[...], is2_ref[...]
    iq = proj(C_IQ, IDX_HEADS * IDX_DIM)
    for blk in range(IDX_HEADS * IDX_DIM // LANES):
        sl = slice(blk * LANES, (blk + 1) * LANES)
        iq_ref[0, :, sl] = _rope128(iq[:, sl], ic, is1, is2, i_half).astype(BF16)

    sm = proj(C_SM, LANES)
    sm_ref[0] = sm * sms_ref[...] + smb_ref[...]
    lane = lax.broadcasted_iota(I32, (1, LANES), 1)
    is_k = lane < IDX_DIM
    mu = jnp.sum(jnp.where(is_k, sm, 0.0), axis=-1, keepdims=True) * (1.0 / IDX_DIM)
    dv = jnp.where(is_k, sm - mu, 0.0)
    var = jnp.sum(dv * dv, axis=-1, keepdims=True) * (1.0 / IDX_DIM)
    ikn = dv * lax.rsqrt(var + NORM_EPS) * lng_ref[...] + lnb_ref[...]
    ik_ref[0] = _rope128(ikn, ic, is1, is2, i_half)[:, :IDX_DIM].astype(BF16)


def _inproj(x, mod, g_pre, w_in, conv_w, conv_b, b_igate, b_fgate, g_idx_k, b_idx_k):
    bsz, seq, d = x.shape
    ts = min(ROW_TILE, seq)
    o_mi = 4 * M_WIDTH
    o_aq = o_mi + 2 * M_HEADS
    o_iw = o_aq + A_WIDTH + 2 * A_HEAD_DIM + IDX_HEADS * IDX_DIM + IDX_DIM
    w = jnp.concatenate([w_in[:, :o_mi], w_in[:, o_aq:o_iw], w_in[:, o_mi:o_aq], w_in[:, o_iw:],
                         jnp.zeros((d, IN_PAD - w_in.shape[1]), w_in.dtype)], axis=1).astype(BF16)
    idx_w_scale = (IDX_HEADS ** -0.5) * (IDX_DIM ** -0.5)
    sm_scale = jnp.ones((LANES,), F32).at[SM_IW:SM_IW + IDX_HEADS].set(idx_w_scale)
    sm_bias = jnp.zeros((LANES,), F32).at[SM_MI:SM_MI + M_HEADS].set(b_igate).at[SM_MF:SM_MF + M_HEADS].set(b_fgate)
    ln_g = jnp.zeros((LANES,), F32).at[:IDX_DIM].set(g_idx_k)
    ln_b = jnp.zeros((LANES,), F32).at[:IDX_DIM].set(b_idx_k)
    ac, as1, as2, a_half = _rope_tables(seq, A_HEAD_DIM, A_HEAD_DIM)
    ic, is1, is2, i_half = _rope_tables(seq, IDX_DIM, IDX_DIM)

    row = lambda width: pl.BlockSpec((1, ts, width), lambda b, s: (b, s, 0))
    vec = lambda width: pl.BlockSpec((1, width), lambda b, s: (0, 0))
    tab = pl.BlockSpec((ts, LANES), lambda b, s: (s, 0))
    outs = [(M_WIDTH, BF16), (M_WIDTH, BF16), (M_WIDTH, BF16), (M_WIDTH, F32), (A_WIDTH, BF16),
            (A_HEAD_DIM, BF16), (A_HEAD_DIM, BF16), (IDX_HEADS * IDX_DIM, BF16), (IDX_DIM, BF16), (LANES, F32)]
    return pl.pallas_call(
        functools.partial(_inproj_kernel, a_half=a_half, i_half=i_half),
        out_shape=[jax.ShapeDtypeStruct((bsz, seq, wd), dt) for wd, dt in outs],
        grid=(bsz, seq // ts),
        in_specs=[
            row(d),
            pl.BlockSpec((1, 6, d), lambda b, s: (b, 0, 0)),
            vec(d),
            pl.BlockSpec((d, IN_PAD), lambda b, s: (0, 0)),
            pl.BlockSpec((CONV_WIDTH, 2 * M_WIDTH), lambda b, s: (0, 0)),
            vec(2 * M_WIDTH), vec(LANES), vec(LANES), vec(LANES), vec(LANES),
            tab, tab, tab, tab, tab, tab,
        ],
        out_specs=[row(wd) for wd, _ in outs],
        scratch_shapes=[pltpu.VMEM((ts + 2 * SUBLANES, 2 * M_WIDTH), F32)],
        compiler_params=_params(("arbitrary", "arbitrary")),
        name="inproj",
    )(x, mod, g_pre.reshape(1, d), w, conv_w, conv_b.reshape(1, -1), ln_g.reshape(1, -1), ln_b.reshape(1, -1),
      sm_scale.reshape(1, -1), sm_bias.reshape(1, -1), ac, as1, as2, ic, is1, is2)


def _mlstm_kernel(q_ref, k_ref, v_ref, o_ref, gc_ref, gr_ref, gh_ref, h_ref, c_st, n_st, m_st):
    L = q_ref.shape[1]

    @pl.when(pl.program_id(1) == 0)
    def _():
        c_st[...] = jnp.zeros(c_st.shape, F32)
        n_st[...] = jnp.zeros(n_st.shape, F32)
        m_st[...] = jnp.full(m_st.shape, STAB_INIT, F32)

    row_i = lax.broadcasted_iota(I32, (L, L), 0)
    col_i = lax.broadcasted_iota(I32, (L, L), 1)
    tril = row_i >= col_i
    tri_l = jnp.where(tril, 1.0, 0.0).astype(BF16)
    tri_u = jnp.where(row_i <= col_i, 1.0, 0.0).astype(BF16)

    gc = gc_ref[0]
    gr = gr_ref[0, 0]
    lf_c = _log_sigmoid(gc)
    lf_r = _log_sigmoid(gr)
    c_hi, c_lo = _split_bf16(lf_c)
    c_lo2 = (lf_c - c_hi.astype(F32) - c_lo.astype(F32)).astype(BF16)
    b_c = _dot(tri_l, c_hi) + (_dot(tri_l, c_lo) + _dot(tri_l, c_lo2))
    r_hi, r_lo = _split_bf16(lf_r)
    r_lo2 = (lf_r - r_hi.astype(F32) - r_lo.astype(F32)).astype(BF16)
    b_r = _dot(r_hi, tri_u) + (_dot(r_lo, tri_u) + _dot(r_lo2, tri_u))

    for hd in range(M_HEADS):
        sl = slice(hd * M_HEAD_DIM, (hd + 1) * M_HEAD_DIM)
        q = q_ref[0, :, sl]
        k = k_ref[0, :, sl]
        v = v_ref[0, :, sl]
        i_col = gc[:, SM_MI + hd:SM_MI + hd + 1]
        b_col = b_c[:, SM_MF + hd:SM_MF + hd + 1]
        i_row = gr[hd:hd + 1, :]
        b_row = b_r[M_HEADS + hd:M_HEADS + hd + 1, :]
        m_prev = m_st[hd][:, 0:1]
        c_prev = c_st[hd]
        n_prev = n_st[hd]

        log_d = jnp.where(tril, (b_col - b_row) + i_row, -jnp.inf)
        m_inter = b_col + m_prev
        m_t = jnp.maximum(m_inter, jnp.max(log_d, axis=-1, keepdims=True))
        dmat = jnp.exp(log_d - m_t)
        s_qk = _dot_nt(q, k) * dmat
        inter = jnp.exp(m_inter - m_t)
        qf = q.astype(F32)
        num = _dot(s_qk.astype(BF16), v) + inter * _dot_nt(q, c_prev.astype(BF16))
        den = jnp.sum(s_qk, axis=-1, keepdims=True) + inter * jnp.sum(qf * n_prev, axis=-1, keepdims=True)
        hh = num / jnp.maximum(jnp.abs(den), jnp.exp(-m_t))

        b_last = b_col[L - 1:L, :]
        log_w = (b_last - b_col) + i_col
        m_new = jnp.maximum(b_last + m_prev, jnp.max(log_w, axis=0, keepdims=True))
        w = jnp.exp(log_w - m_new)
        decay = jnp.exp((b_last + m_prev) - m_new)
        vw = (v.astype(F32) * w).astype(BF16)
        c_st[hd] = decay * c_prev + lax.dot_general(vw, k, (((0,), (0,)), ((), ())), preferred_element_type=F32)
        n_st[hd] = decay * n_prev + jnp.sum(w * k.astype(F32), axis=0, keepdims=True)
        m_st[hd] = jnp.broadcast_to(m_new, (1, LANES))

        hn = hh * lax.rsqrt(jnp.mean(hh * hh, axis=-1, keepdims=True) + NORM_EPS) * gh_ref[:, sl]
        h_ref[0, :, sl] = (_sigmoid(o_ref[0, :, sl]) * hn).astype(BF16)


def _mlstm(qm, km, vm, om, small, g_head):
    bsz, seq, _ = qm.shape
    L = min(MLSTM_CHUNK, seq)
    nc = seq // L
    gr = small[:, :, SM_MI:SM_MI + 2 * M_HEADS].reshape(bsz, nc, L, 2 * M_HEADS).transpose(0, 1, 3, 2)
    row = lambda width: pl.BlockSpec((1, L, width), lambda b, c: (b, c, 0))
    return pl.pallas_call(
        _mlstm_kernel,
        out_shape=jax.ShapeDtypeStruct((bsz, seq, M_WIDTH), BF16),
        grid=(bsz, nc),
        in_specs=[row(M_WIDTH), row(M_WIDTH), row(M_WIDTH), row(M_WIDTH), row(LANES),
                  pl.BlockSpec((1, 1, 2 * M_HEADS, L), lambda b, c: (b, c, 0, 0)),
                  pl.BlockSpec((1, M_WIDTH), lambda b, c: (0, 0))],
        out_specs=row(M_WIDTH),
        scratch_shapes=[pltpu.VMEM((M_HEADS, M_HEAD_DIM, M_HEAD_DIM), F32),
                        pltpu.VMEM((M_HEADS, 1, M_HEAD_DIM), F32),
                        pltpu.VMEM((M_HEADS, 1, LANES), F32)],
        compiler_params=_params(("arbitrary", "arbitrary")),
        name="mlstm",
    )(qm, km, vm, om, small, gr, g_head.reshape(1, -1))


def _sortable(bits):
    return bits ^ ((bits >> 31) & jnp.int32(0x7FFFFFFF))


_KEY_NEG_INF = int(np.array([-np.inf], np.float32).view(np.int32)[0]) ^ 0x7FFFFFFF
_KEY_NEG_INF = _KEY_NEG_INF - (1 << 32) if _KEY_NEG_INF >= (1 << 31) else _KEY_NEG_INF
_INT_MIN = -(1 << 31)
_NEG_BIG = -1e30


def _dsa_kernel(iq_ref, sm_ref, aq_ref, ik_ref, ak_ref, av_ref, o_ref, keys_ref, acc_ref, m_ref, l_ref,
                *, k_top, kt, seq):
    qb = pl.program_id(1)
    nq = iq_ref.shape[1]
    nkt = (qb * nq) // kt + 1
    row = lax.broadcasted_iota(I32, (nq, 1), 0)
    lim = qb * nq + (row // CHUNK + 1) * CHUNK
    lane = lax.broadcasted_iota(I32, (1, kt), 1)
    iq = iq_ref[0]
    sm = sm_ref[0]

    def score_body(j, carry):
        start = pl.multiple_of(j * kt, kt)
        ikt = ik_ref[0, pl.ds(start, kt), :]
        sc = jnp.zeros((nq, kt), F32)
        for hd in range(IDX_HEADS):
            s = _dot_nt(iq[:, hd * IDX_DIM:(hd + 1) * IDX_DIM], ikt)
            sc = sc + sm[:, SM_IW + hd:SM_IW + hd + 1] * jnp.maximum(s, 0.0)
        key = _sortable(pltpu.bitcast(sc, I32))
        keys_ref[j] = jnp.where(start + lane < lim, key, _KEY_NEG_INF)
        return carry

    lax.fori_loop(0, nkt, score_body, 0)

    def count(pred):
        def body(j, acc):
            return acc + jnp.sum(jnp.where(pred(keys_ref[j], j), 1.0, 0.0), axis=1, keepdims=True)
        return lax.fori_loop(0, nkt, body, jnp.zeros((nq, 1), F32))

    kf = float(k_top)
    t0 = jnp.where(count(lambda key, j: key >= 0) >= kf, 0, _INT_MIN).astype(I32)

    def bit_body(i, t):
        cand = t | lax.shift_left(jnp.int32(1), 30 - i)
        return jnp.where(count(lambda key, j: key >= cand) >= kf, cand, t)

    t = lax.fori_loop(0, 31, bit_body, t0)
    n_gt = count(lambda key, j: key > t)
    n_eq = count(lambda key, j: key == t)
    need = kf - n_gt
    tie = (n_eq > need) & (t > _KEY_NEG_INF)

    def tie_search():
        p = jnp.zeros((nq, 1), I32)
        for b in reversed(range(max(1, int(seq - 1).bit_length()))):
            cand = p | (1 << b)
            c = count(lambda key, j: (key == t) & (j * kt + lane < cand))
            p = jnp.where(c <= need - 1.0, cand, p)
        return p

    any_tie = jnp.max(jnp.where(tie, 1.0, 0.0)) > 0.0
    p_idx = lax.cond(any_tie, tie_search, lambda: jnp.full((nq, 1), seq, I32))
    p_idx = jnp.where(tie, p_idx, seq)

    acc_ref[...] = jnp.zeros(acc_ref.shape, F32)
    m_ref[...] = jnp.full(m_ref.shape, _NEG_BIG, F32)
    l_ref[...] = jnp.zeros(l_ref.shape, F32)
    aq = aq_ref[0]
    scale = A_HEAD_DIM ** -0.5

    def att_body(j, carry):
        start = pl.multiple_of(j * kt, kt)
        key = keys_ref[j]
        sel = ((key > t) | ((key == t) & (start + lane <= p_idx))) & (key > _KEY_NEG_INF)
        akt = ak_ref[0, pl.ds(start, kt), :]
        avt = av_ref[0, pl.ds(start, kt), :]
        for hd in range(A_HEADS):
            lg = _dot_nt(aq[:, hd * A_HEAD_DIM:(hd + 1) * A_HEAD_DIM], akt) * scale
            lg = jnp.where(sel, lg, _NEG_BIG)
            m_old = m_ref[hd]
            m_new = jnp.maximum(m_old, jnp.max(lg, axis=1, keepdims=True))
            p = jnp.where(sel, jnp.exp(lg - m_new), 0.0)
            alpha = jnp.exp(m_old - m_new)
            l_ref[hd] = alpha * l_ref[hd] + jnp.sum(p, axis=1, keepdims=True)
            acc_ref[hd] = alpha * acc_ref[hd] + _dot(p.astype(BF16), avt)
            m_ref[hd] = m_new
        return carry

    lax.fori_loop(0, nkt, att_body, 0)
    for hd in range(A_HEADS):
        o_ref[0, :, hd * A_HEAD_DIM:(hd + 1) * A_HEAD_DIM] = (acc_ref[hd] / l_ref[hd]).astype(BF16)


def _dsa(iq, ik, small, aq, ak, av):
    bsz, seq, _ = aq.shape
    nq = min(Q_BLOCK, seq)
    kt = min(KEY_TILE, seq)
    k_top = min(TOPK_KEYS_MAX, seq // 4)
    assert kt >= k_top and kt % nq == 0
    qrow = lambda width: pl.BlockSpec((1, nq, width), lambda b, q: (b, q, 0))
    full = lambda width: pl.BlockSpec((1, seq, width), lambda b, q: (b, 0, 0))
    return pl.pallas_call(
        functools.partial(_dsa_kernel, k_top=k_top, kt=kt, seq=seq),
        out_shape=jax.ShapeDtypeStruct((bsz, seq, A_WIDTH), BF16),
        grid=(bsz, seq // nq),
        in_specs=[qrow(IDX_HEADS * IDX_DIM), qrow(LANES), qrow(A_WIDTH),
                  full(IDX_DIM), full(A_HEAD_DIM), full(A_HEAD_DIM)],
        out_specs=qrow(A_WIDTH),
        scratch_shapes=[pltpu.VMEM((seq // kt, nq, kt), I32),
                        pltpu.VMEM((A_HEADS, nq, A_HEAD_DIM), F32),
                        pltpu.VMEM((A_HEADS, nq, 1), F32),
                        pltpu.VMEM((A_HEADS, nq, 1), F32)],
        compiler_params=_params(("arbitrary", "arbitrary")),
        name="dsa",
    )(iq, small, aq, ik, ak, av)


def _pack_bf16_pair(lo, hi):
    lo_b = pltpu.bitcast(lo.astype(BF16).astype(F32), U32)
    hi_b = pltpu.bitcast(hi.astype(BF16).astype(F32), U32)
    return (lo_b >> 16) | (hi_b & jnp.uint32(0xFFFF0000))


def _unpack_bf16_pair(w):
    lo = pltpu.bitcast(w << 16, F32)
    hi = pltpu.bitcast(w & jnp.uint32(0xFFFF0000), F32)
    return lo, hi


def _outproj_kernel(hm_ref, ha_ref, x_ref, mod_ref, gpost_ref, gpre_ref, wo_ref, wr_ref,
                    x1_ref, h2_ref, lt_ref):
    gate = mod_ref[0, 2:3, :]
    shift2 = mod_ref[0, 3:4, :]
    scale2 = mod_ref[0, 4:5, :]
    y = _dot(hm_ref[0], wo_ref[0:M_WIDTH, :]) + _dot(ha_ref[0], wo_ref[M_WIDTH:, :])
    yn = y * lax.rsqrt(jnp.mean(y * y, axis=-1, keepdims=True) + NORM_EPS) * gpost_ref[...]
    x1 = x_ref[0] + gate * yn
    x1_ref[0] = x1
    xn = x1 * lax.rsqrt(jnp.mean(x1 * x1, axis=-1, keepdims=True) + NORM_EPS) * gpre_ref[...]
    h2 = xn * (1.0 + scale2) + shift2
    half = h2.shape[1] // 2
    h2_ref[0] = _pack_bf16_pair(h2[:, :half], h2[:, half:])
    hh, hl = _split_bf16(h2)
    wh, wl = _split_bf16(wr_ref[...])
    lt_ref[0] = _dot_nt(wh, hh) + (_dot_nt(wh, hl) + _dot_nt(wl, hh))


def _outproj(hm, ha, x, mod, g_post, g_ffn_pre, w_out, w_router):
    bsz, seq, d = x.shape
    ts = min(ROW_TILE, seq)
    row = lambda width: pl.BlockSpec((1, ts, width), lambda b, s: (b, s, 0))
    vec = pl.BlockSpec((1, d), lambda b, s: (0, 0))
    x1, h2p, lt = pl.pallas_call(
        _outproj_kernel,
        out_shape=[jax.ShapeDtypeStruct((bsz, seq, d), F32),
                   jax.ShapeDtypeStruct((bsz, seq, d // 2), U32),
                   jax.ShapeDtypeStruct((bsz, N_EXPERTS, seq), F32)],
        grid=(bsz, seq // ts),
        in_specs=[row(M_WIDTH), row(A_WIDTH), row(d),
                  pl.BlockSpec((1, 6, d), lambda b, s: (b, 0, 0)), vec, vec,
                  pl.BlockSpec((d, d), lambda b, s: (0, 0)),
                  pl.BlockSpec((N_EXPERTS, d), lambda b, s: (0, 0))],
        out_specs=[row(d), row(d // 2), pl.BlockSpec((1, N_EXPERTS, ts), lambda b, s: (b, 0, s))],
        compiler_params=_params(("arbitrary", "arbitrary")),
        name="outproj",
    )(hm, ha, x, mod, g_post.reshape(1, d), g_ffn_pre.reshape(1, d), w_out.astype(BF16), w_router.T)
    return x1, h2p, lt


def _route_kernel(lt_ref, br_ref, eid_ref, rnk_ref, gw_ref, cnt_ref, carry):
    first = (pl.program_id(0) == 0) & (pl.program_id(1) == 0)

    @pl.when(first)
    def _():
        carry[...] = jnp.zeros(carry.shape, F32)

    tsr = lt_ref.shape[2]
    scores = _sigmoid(lt_ref[0])
    sel = scores + br_ref[...]
    x3 = sel.reshape(N_GROUPS, GROUP_SIZE, tsr)
    io3 = lax.broadcasted_iota(I32, x3.shape, 1)
    m1 = jnp.max(x3, axis=1, keepdims=True)
    i1 = jnp.min(jnp.where(x3 == m1, io3, GROUP_SIZE), axis=1, keepdims=True)
    m2 = jnp.max(jnp.where(io3 == i1, -jnp.inf, x3), axis=1, keepdims=True)
    gs = (m1 + m2).reshape(N_GROUPS, tsr)
    iog = lax.broadcasted_iota(I32, gs.shape, 0)
    g_rank = jnp.zeros(gs.shape, F32)
    for gp in range(N_GROUPS):
        r = gs[gp:gp + 1, :]
        g_rank = g_rank + jnp.where((r > gs) | ((r == gs) & (iog > gp)), 1.0, 0.0)
    g_sel = g_rank < float(TOPK_GROUPS)
    e_mask = jnp.broadcast_to(g_sel.reshape(N_GROUPS, 1, tsr), x3.shape).reshape(N_EXPERTS, tsr)
    v = jnp.where(e_mask, sel, -jnp.inf)
    ioe = lax.broadcasted_iota(I32, v.shape, 0)
    e_rank = jnp.zeros(v.shape, F32)
    for ep in range(N_EXPERTS):
        r = v[ep:ep + 1, :]
        e_rank = e_rank + jnp.where((r > v) | ((r == v) & (ioe > ep)), 1.0, 0.0)
    e_sel = e_rank < float(TOPK_EXPERTS)
    picked = jnp.where(e_sel, scores, 0.0)
    wd = picked / jnp.sum(picked, axis=0, keepdims=True) * ROUTE_SCALE

    sel_b = jnp.where(e_sel, 1.0, 0.0).astype(BF16)
    r_i = lax.broadcasted_iota(I32, (tsr, tsr), 0)
    c_i = lax.broadcasted_iota(I32, (tsr, tsr), 1)
    incl = _dot(sel_b, jnp.where(r_i <= c_i, 1.0, 0.0).astype(BF16))
    rank = carry[...] + incl - sel_b.astype(F32)
    carry[...] = carry[...] + incl[:, tsr - 1:tsr]
    cnt_ref[...] = jnp.broadcast_to(carry[...], cnt_ref.shape)
    e_r = lax.broadcasted_iota(I32, (N_EXPERTS, N_EXPERTS), 0)
    e_c = lax.broadcasted_iota(I32, (N_EXPERTS, N_EXPERTS), 1)
    slot = _dot(jnp.where(e_c < e_r, 1.0, 0.0).astype(BF16), sel_b)
    ioe_f = ioe.astype(F32)
    for kk in range(TOPK_EXPERTS):
        mk = e_sel & (slot == float(kk))
        eid_ref[0, kk:kk + 1, :] = jnp.sum(jnp.where(mk, ioe_f, 0.0), axis=0, keepdims=True).astype(I32)
        rnk_ref[0, kk:kk + 1, :] = jnp.sum(jnp.where(mk, rank, 0.0), axis=0, keepdims=True).astype(I32)
        gw_ref[0, kk:kk + 1, :] = jnp.sum(jnp.where(mk, wd, 0.0), axis=0, keepdims=True)


def _route(lt, b_router):
    bsz, _, seq = lt.shape
    tsr = min(ROUTE_TILE, seq)
    slab = pl.BlockSpec((1, TOPK_EXPERTS, tsr), lambda b, s: (b, 0, s))
    return pl.pallas_call(
        _route_kernel,
        out_shape=[jax.ShapeDtypeStruct((bsz, TOPK_EXPERTS, seq), I32),
                   jax.ShapeDtypeStruct((bsz, TOPK_EXPERTS, seq), I32),
                   jax.ShapeDtypeStruct((bsz, TOPK_EXPERTS, seq), F32),
                   jax.ShapeDtypeStruct((N_EXPERTS, LANES), F32)],
        grid=(bsz, seq // tsr),
        in_specs=[pl.BlockSpec((1, N_EXPERTS, tsr), lambda b, s: (b, 0, s)),
                  pl.BlockSpec((N_EXPERTS, 1), lambda b, s: (0, 0))],
        out_specs=[slab, slab, slab, pl.BlockSpec((N_EXPERTS, LANES), lambda b, s: (0, 0))],
        scratch_shapes=[pltpu.VMEM((N_EXPERTS, 1), F32)],
        compiler_params=_params(("arbitrary", "arbitrary")),
        name="route",
    )(lt, b_router.reshape(N_EXPERTS, 1))


def _dispatch_kernel(pos_ref, h2_ref, xs_in_ref, xs_ref, sem):
    del xs_in_ref
    tsd = h2_ref.shape[1]

    def row_copy(t, kk):
        return pltpu.make_async_copy(h2_ref.at[0, pl.ds(t, 1)], xs_ref.at[pl.ds(pos_ref[0, kk, t], 1)], sem)

    def body(t, carry):
        for kk in range(TOPK_EXPERTS):
            row_copy(t, kk).start()
        return carry

    lax.fori_loop(0, tsd, body, 0)
    for kk in range(TOPK_EXPERTS):
        pltpu.make_async_copy(h2_ref.at[0], xs_ref.at[pl.ds(0, tsd)], sem).wait()


def _dispatch(pos, h2p, rows):
    bsz, seq, dw = h2p.shape
    tsd = min(DISPATCH_TILE, seq)
    xs0 = jnp.zeros((rows, dw), U32)
    return pl.pallas_call(
        _dispatch_kernel,
        out_shape=jax.ShapeDtypeStruct((rows, dw), U32),
        grid=(bsz, seq // tsd),
        in_specs=[pl.BlockSpec((1, TOPK_EXPERTS, tsd), lambda b, s: (b, 0, s), memory_space=pltpu.SMEM),
                  pl.BlockSpec((1, tsd, dw), lambda b, s: (b, s, 0)),
                  pl.BlockSpec(memory_space=pl.ANY)],
        out_specs=pl.BlockSpec(memory_space=pl.ANY),
        scratch_shapes=[pltpu.SemaphoreType.DMA],
        input_output_aliases={2: 0},
        compiler_params=_params(("arbitrary", "arbitrary"), has_side_effects=True),
        name="dispatch",
    )(pos, h2p, xs0)


def _swiglu_packed(xp, wg_ref, wu_ref, wd_ref):
    lo, hi = _unpack_bf16_pair(xp)
    half = lo.shape[1]
    lo, hi = lo.astype(BF16), hi.astype(BF16)
    g = _dot(lo, wg_ref[0:half, :]) + _dot(hi, wg_ref[half:, :])
    u = _dot(lo, wu_ref[0:half, :]) + _dot(hi, wu_ref[half:, :])
    a = (_silu(g) * u).astype(BF16)
    return _dot(a, wd_ref[...])


def _expert_kernel(be_ref, xs_ref, wg_ref, wu_ref, wd_ref, ys_ref):
    del be_ref
    y = _swiglu_packed(xs_ref[...], wg_ref.at[0], wu_ref.at[0], wd_ref.at[0])
    half = y.shape[1] // 2
    ys_ref[...] = _pack_bf16_pair(y[:, :half], y[:, half:])


def _experts(block_e, xs, w_gate, w_up, w_down):
    rows, dw = xs.shape
    _, d, f = w_gate.shape
    nb = rows // EXPERT_ROWS
    return pl.pallas_call(
        _expert_kernel,
        out_shape=jax.ShapeDtypeStruct((rows, dw), U32),
        grid_spec=pltpu.PrefetchScalarGridSpec(
            num_scalar_prefetch=1,
            grid=(nb,),
            in_specs=[pl.BlockSpec((EXPERT_ROWS, dw), lambda i, be: (i, 0)),
                      pl.BlockSpec((1, d, f), lambda i, be: (be[i], 0, 0)),
                      pl.BlockSpec((1, d, f), lambda i, be: (be[i], 0, 0)),
                      pl.BlockSpec((1, f, d), lambda i, be: (be[i], 0, 0))],
            out_specs=pl.BlockSpec((EXPERT_ROWS, dw), lambda i, be: (i, 0)),
        ),
        compiler_params=_params(("arbitrary",)),
        name="experts",
    )(block_e, xs, w_gate.astype(BF16), w_up.astype(BF16), w_down.astype(BF16))


def _combine_kernel(pos_ref, gw_ref, x1_ref, h2_ref, mod_ref, g_ref, wsg_ref, wsu_ref, wsd_ref, ys_ref,
                    o_ref, buf, sem):
    tsc = h2_ref.shape[1]

    def row_copy(t, kk):
        return pltpu.make_async_copy(ys_ref.at[pl.ds(pos_ref[0, kk, t], 1)], buf.at[kk, pl.ds(t, 1)], sem)

    def body(t, carry):
        for kk in range(TOPK_EXPERTS):
            row_copy(t, kk).start()
        return carry

    lax.fori_loop(0, tsc, body, 0)
    shared = _swiglu_packed(h2_ref[0], wsg_ref, wsu_ref, wsd_ref)
    for kk in range(TOPK_EXPERTS):
        pltpu.make_async_copy(ys_ref.at[pl.ds(0, tsc)], buf.at[kk], sem).wait()

    half = shared.shape[1] // 2
    gw = gw_ref[0]
    y_lo, y_hi = shared[:, :half], shared[:, half:]
    for kk in range(TOPK_EXPERTS):
        lo, hi = _unpack_bf16_pair(buf[kk])
        w = gw[:, kk:kk + 1]
        y_lo = y_lo + w * lo
        y_hi = y_hi + w * hi
    ms = (jnp.sum(y_lo * y_lo, axis=-1, keepdims=True) + jnp.sum(y_hi * y_hi, axis=-1, keepdims=True)) * (0.5 / half)
    inv = lax.rsqrt(ms + NORM_EPS)
    gate = mod_ref[0, 5:6, :]
    g = g_ref[...]
    o_ref[0, :, :half] = x1_ref[0, :, :half] + gate[:, :half] * (y_lo * inv * g[:, :half])
    o_ref[0, :, half:] = x1_ref[0, :, half:] + gate[:, half:] * (y_hi * inv * g[:, half:])


def _combine(pos, gw_t, x1, h2p, mod, g_post, ws_gate, ws_up, ws_down, ys):
    bsz, seq, d = x1.shape
    dw = h2p.shape[2]
    f = ws_gate.shape[1]
    tsc = min(COMBINE_TILE, seq)
    row = lambda width: pl.BlockSpec((1, tsc, width), lambda b, s: (b, s, 0))
    const = lambda shape: pl.BlockSpec(shape, lambda b, s: tuple(0 for _ in shape))
    return pl.pallas_call(
        _combine_kernel,
        out_shape=jax.ShapeDtypeStruct((bsz, seq, d), F32),
        grid=(bsz, seq // tsc),
        in_specs=[pl.BlockSpec((1, TOPK_EXPERTS, tsc), lambda b, s: (b, 0, s), memory_space=pltpu.SMEM),
                  row(TOPK_EXPERTS), row(d), row(dw),
                  pl.BlockSpec((1, 6, d), lambda b, s: (b, 0, 0)),
                  const((1, d)), const((d, f)), const((d, f)), const((f, d)),
                  pl.BlockSpec(memory_space=pl.ANY)],
        out_specs=row(d),
        scratch_shapes=[pltpu.VMEM((TOPK_EXPERTS, tsc, dw), U32), pltpu.SemaphoreType.DMA],
        compiler_params=_params(("arbitrary", "arbitrary")),
        name="combine",
    )(pos, gw_t, x1, h2p, mod, g_post.reshape(1, d), ws_gate.astype(BF16), ws_up.astype(BF16),
      ws_down.astype(BF16), ys)


def _layer(x, c, w_ada, b_ada, g_mix_pre, g_mix_post, g_ffn_pre, g_ffn_post, w_in, conv_w, conv_b, b_igate, b_fgate,
           g_mlstm_head, g_idx_k, b_idx_k, w_out, w_router, b_router, w_exp_gate, w_exp_up, w_exp_down,
           w_sh_gate, w_sh_up, w_sh_down):
    bsz, seq, d = x.shape
    mod = _adaln(c, w_ada, b_ada).reshape(bsz, 6, d)
    qm, km, vm, om, aq, ak, av, iq, ik, small = _inproj(x, mod, g_mix_pre, w_in, conv_w, conv_b, b_igate, b_fgate,
                                                        g_idx_k, b_idx_k)
    hm = _mlstm(qm, km, vm, om, small, g_mlstm_head)
    ha = _dsa(iq, ik, small, aq, ak, av)
    x1, h2p, lt = _outproj(hm, ha, x, mod, g_mix_post, g_ffn_pre, w_out, w_router)

    eid, rnk, gw, cnt = _route(lt, b_router)
    counts = cnt[:, 0].astype(I32)
    padded = (counts + EXPERT_ROWS - 1) // EXPERT_ROWS * EXPERT_ROWS
    pad_end = jnp.cumsum(padded)
    pad_start = pad_end - padded
    pos = pad_start[eid] + rnk
    nb = (bsz * seq * TOPK_EXPERTS + EXPERT_ROWS - 1) // EXPERT_ROWS + N_EXPERTS
    block_e = jnp.clip(jnp.searchsorted(pad_end, jnp.arange(nb, dtype=I32) * EXPERT_ROWS, side="right"),
                       0, N_EXPERTS - 1).astype(I32)

    xs = _dispatch(pos, h2p, nb * EXPERT_ROWS)
    ys = _experts(block_e, xs, w_exp_gate, w_exp_up, w_exp_down)
    return _combine(pos, gw.transpose(0, 2, 1), x1, h2p, mod, g_ffn_post, w_sh_gate, w_sh_up, w_sh_down, ys)


def kernel(x, c, w_ada, b_ada, g_mix_pre, g_mix_post, g_ffn_pre, g_ffn_post, w_in, conv_w, conv_b, b_igate, b_fgate, g_mlstm_head, g_idx_k, b_idx_k, w_out, w_router, b_router, w_exp_gate, w_exp_up, w_exp_down, w_sh_gate, w_sh_up, w_sh_down):
    per_layer = (w_ada, b_ada, g_mix_pre, g_mix_post, g_ffn_pre, g_ffn_post, w_in, conv_w, conv_b, b_igate, b_fgate,
                 g_mlstm_head, g_idx_k, b_idx_k, w_out, w_router, b_router, w_exp_gate, w_exp_up, w_exp_down,
                 w_sh_gate, w_sh_up, w_sh_down)
    for layer in range(w_ada.shape[0]):
        x = _layer(x, c, *(p[layer] for p in per_layer))
    return x
```

```python
import functools

import jax
import jax.numpy as jnp
import numpy as np
from jax import lax
from jax.experimental import pallas as pl
from jax.experimental.pallas import tpu as pltpu

F32 = jnp.float32
BF16 = jnp.bfloat16
I32 = jnp.int32
U32 = jnp.uint32

CHUNK = 64
M_HEADS = 4
M_HEAD_DIM = 128
M_WIDTH = M_HEADS * M_HEAD_DIM
CONV_WIDTH = 4
A_HEADS = 4
A_HEAD_DIM = 128
A_WIDTH = A_HEADS * A_HEAD_DIM
IDX_HEADS = 4
IDX_DIM = 64
TOPK_KEYS_MAX = 256
Q_BLOCK = 128
ROPE_THETA = 500000.0
ROPE_FRACTION_DIV = 4
N_EXPERTS = 64
N_GROUPS = 8
GROUP_SIZE = N_EXPERTS // N_GROUPS
TOPK_GROUPS = 4
TOPK_EXPERTS = 8
D_EXPERT = 256
ROUTE_SCALE = 2.5
NORM_EPS = 1e-6
STAB_INIT = -1e30

LANES = 128
SUBLANES = 8
VMEM_LIMIT = 56 * 1024 * 1024

ROW_TILE = 512
MLSTM_CHUNK = 256
ROUTE_TILE = 512
EXPERT_ROWS = 512
DISPATCH_TILE = 256
COMBINE_TILE = 256

C_MQ = 0
C_MV = 2 * M_WIDTH
C_MO = 3 * M_WIDTH
C_AQ = 4 * M_WIDTH
C_AK = C_AQ + A_WIDTH
C_AV = C_AK + A_HEAD_DIM
C_IQ = C_AV + A_HEAD_DIM
C_SM = C_IQ + IDX_HEADS * IDX_DIM
IN_PAD = C_SM + LANES
SM_MI = IDX_DIM
SM_MF = IDX_DIM + M_HEADS
SM_IW = IDX_DIM + 2 * M_HEADS


def _params(sem, **kw):
    return pltpu.CompilerParams(dimension_semantics=sem, vmem_limit_bytes=VMEM_LIMIT, **kw)


def _split_bf16(x):
    hi = x.astype(BF16)
    lo = (x - hi.astype(F32)).astype(BF16)
    return hi, lo


def _dot(a, b):
    return jnp.dot(a, b, preferred_element_type=F32)


def _dot_nt(a, b):
    return lax.dot_general(a, b, (((1,), (1,)), ((), ())), preferred_element_type=F32)


def _dot3(a, b):
    ah, al = _split_bf16(a)
    bh, bl = _split_bf16(b)
    return _dot(ah, bh) + (_dot(ah, bl) + _dot(al, bh))


def _sigmoid(x):
    return 1.0 / (1.0 + jnp.exp(-x))


def _silu(x):
    return x * _sigmoid(x)


def _log_sigmoid(x):
    return jnp.minimum(x, 0.0) - jnp.log(1.0 + jnp.exp(-jnp.abs(x)))


def _adaln_kernel(c_ref, w_ref, b_ref, o_ref):
    o_ref[...] = _dot3(_silu(c_ref[...]), w_ref[...]) + b_ref[...]


def _adaln(c, w, b):
    bsz, d = c.shape
    n = w.shape[1]
    tn = d
    return pl.pallas_call(
        _adaln_kernel,
        out_shape=jax.ShapeDtypeStruct((bsz, n), F32),
        grid=(n // tn,),
        in_specs=[
            pl.BlockSpec((bsz, d), lambda j: (0, 0)),
            pl.BlockSpec((d, tn), lambda j: (0, j)),
            pl.BlockSpec((1, tn), lambda j: (0, j)),
        ],
        out_specs=pl.BlockSpec((bsz, tn), lambda j: (0, j)),
        compiler_params=_params(("arbitrary",)),
        name="adaln",
    )(c, w, b.reshape(1, n))


def _rope_tables(seq, head_dim, group):
    rot = head_dim // ROPE_FRACTION_DIV
    half = rot // 2
    inv_freq = jnp.exp(-jnp.log(jnp.float32(ROPE_THETA)) * jnp.arange(half, dtype=F32) * (2.0 / rot))
    ang = jnp.arange(seq, dtype=F32)[:, None] * inv_freq[None, :]
    cos, sin = jnp.cos(ang), jnp.sin(ang)
    pad = group - rot
    c = jnp.concatenate([cos, cos, jnp.ones((seq, pad), F32)], axis=1)
    s1 = jnp.concatenate([-sin, jnp.zeros((seq, half + pad), F32)], axis=1)
    s2 = jnp.concatenate([jnp.zeros((seq, half), F32), sin, jnp.zeros((seq, pad), F32)], axis=1)
    rep = LANES // group
    return jnp.tile(c, (1, rep)), jnp.tile(s1, (1, rep)), jnp.tile(s2, (1, rep)), half


def _rope128(x, c, s1, s2, half):
    back = pltpu.roll(x, LANES - half, 1)
    fwd = pltpu.roll(x, half, 1)
    return x * c + back * s1 + fwd * s2


def _inproj_kernel(x_ref, mod_ref, g_ref, w_ref, wavt_ref, cw_ref, cb_ref, lng_ref, lnb_ref, sms_ref, smb_ref,
                   ac_ref, as1_ref, as2_ref, ic_ref, is1_ref, is2_ref,
                   qm_ref, km_ref, vm_ref, om_ref, aq_ref, ak_ref, avt_ref, iq_ref, ik_ref, sm_ref,
                   ubuf, *, a_half, i_half):
    ts = x_ref.shape[1]
    x = x_ref[0]
    shift = mod_ref[0, 0:1, :]
    scale = mod_ref[0, 1:2, :]
    y = x * lax.rsqrt(jnp.mean(x * x, axis=-1, keepdims=True) + NORM_EPS)
    h = (y * g_ref[...]) * (1.0 + scale) + shift
    hb = h.astype(BF16)

    def proj(lo, width):
        return _dot(hb, w_ref[:, lo:lo + width])

    @pl.when(pl.program_id(1) == 0)
    def _():
        ubuf[0:SUBLANES, :] = jnp.zeros((SUBLANES, 2 * M_WIDTH), F32)

    ubuf[SUBLANES:SUBLANES + ts, :] = proj(C_MQ, 2 * M_WIDTH)
    acc = cb_ref[...] + cw_ref[0:1, :] * ubuf[SUBLANES - 3:SUBLANES - 3 + ts, :]
    for j in range(1, CONV_WIDTH):
        off = SUBLANES - (CONV_WIDTH - 1) + j
        acc = acc + cw_ref[j:j + 1, :] * ubuf[off:off + ts, :]
    ubuf[0:SUBLANES, :] = ubuf[ts:ts + SUBLANES, :]
    qk = _silu(acc)
    qm_ref[0] = qk[:, :M_WIDTH].astype(BF16)
    km_ref[0] = (qk[:, M_WIDTH:] * (M_HEAD_DIM ** -0.5)).astype(BF16)

    vm_ref[0] = proj(C_MV, M_WIDTH).astype(BF16)
    om_ref[0] = proj(C_MO, M_WIDTH)

    ac, as1, as2 = ac_ref[...], as1_ref[...], as2_ref[...]
    aq = proj(C_AQ, A_WIDTH)
    for hd in range(A_HEADS):
        sl = slice(hd * A_HEAD_DIM, (hd + 1) * A_HEAD_DIM)
        aq_ref[0, :, sl] = _rope128(aq[:, sl], ac, as1, as2, a_half).astype(BF16)
    ak_ref[0] = _rope128(proj(C_AK, A_HEAD_DIM), ac, as1, as2, a_half).astype(BF16)
    avt_ref[0, 0] = _dot_nt(wavt_ref[...], hb).astype(BF16)

    ic, is1, is2 = ic_ref[...], is1_ref[...], is2_ref[...]
    iq = proj(C_IQ, IDX_HEADS * IDX_DIM)
    for blk in range(IDX_HEADS * IDX_DIM // LANES):
        sl = slice(blk * LANES, (blk + 1) * LANES)
        iq_ref[0, :, sl] = _rope128(iq[:, sl], ic, is1, is2, i_half).astype(BF16)

    sm = proj(C_SM, LANES)
    sm_ref[0] = sm * sms_ref[...] + smb_ref[...]
    lane = lax.broadcasted_iota(I32, (1, LANES), 1)
    is_k = lane < IDX_DIM
    mu = jnp.sum(jnp.where(is_k, sm, 0.0), axis=-1, keepdims=True) * (1.0 / IDX_DIM)
    dv = jnp.where(is_k, sm - mu, 0.0)
    var = jnp.sum(dv * dv, axis=-1, keepdims=True) * (1.0 / IDX_DIM)
    ikn = dv * lax.rsqrt(var + NORM_EPS) * lng_ref[...] + lnb_ref[...]
    ik_ref[0] = _rope128(ikn, ic, is1, is2, i_half)[:, :IDX_DIM].astype(BF16)


def _inproj(x, mod, g_pre, w_in, conv_w, conv_b, b_igate, b_fgate, g_idx_k, b_idx_k):
    bsz, seq, d = x.shape
    ts = min(ROW_TILE, seq)
    o_mi = 4 * M_WIDTH
    o_aq = o_mi + 2 * M_HEADS
    o_iw = o_aq + A_WIDTH + 2 * A_HEAD_DIM + IDX_HEADS * IDX_DIM + IDX_DIM
    w = jnp.concatenate([w_in[:, :o_mi], w_in[:, o_aq:o_iw], w_in[:, o_mi:o_aq], w_in[:, o_iw:],
                         jnp.zeros((d, IN_PAD - w_in.shape[1]), w_in.dtype)], axis=1).astype(BF16)
    idx_w_scale = (IDX_HEADS ** -0.5) * (IDX_DIM ** -0.5)
    sm_scale = jnp.ones((LANES,), F32).at[SM_IW:SM_IW + IDX_HEADS].set(idx_w_scale)
    sm_bias = jnp.zeros((LANES,), F32).at[SM_MI:SM_MI + M_HEADS].set(b_igate).at[SM_MF:SM_MF + M_HEADS].set(b_fgate)
    ln_g = jnp.zeros((LANES,), F32).at[:IDX_DIM].set(g_idx_k)
    ln_b = jnp.zeros((LANES,), F32).at[:IDX_DIM].set(b_idx_k)
    ac, as1, as2, a_half = _rope_tables(seq, A_HEAD_DIM, A_HEAD_DIM)
    ic, is1, is2, i_half = _rope_tables(seq, IDX_DIM, IDX_DIM)

    row = lambda width: pl.BlockSpec((1, ts, width), lambda b, s: (b, s, 0))
    vec = lambda width: pl.BlockSpec((1, width), lambda b, s: (0, 0))
    tab = pl.BlockSpec((ts, LANES), lambda b, s: (s, 0))
    outs = [(M_WIDTH, BF16), (M_WIDTH, BF16), (M_WIDTH, BF16), (M_WIDTH, F32), (A_WIDTH, BF16),
            (A_HEAD_DIM, BF16), None, (IDX_HEADS * IDX_DIM, BF16), (IDX_DIM, BF16), (LANES, F32)]
    avt_shape = jax.ShapeDtypeStruct((bsz, seq // ts, A_HEAD_DIM, ts), BF16)
    avt_spec = pl.BlockSpec((1, 1, A_HEAD_DIM, ts), lambda b, s: (b, s, 0, 0))
    return pl.pallas_call(
        functools.partial(_inproj_kernel, a_half=a_half, i_half=i_half),
        out_shape=[avt_shape if o is None else jax.ShapeDtypeStruct((bsz, seq, o[0]), o[1]) for o in outs],
        grid=(bsz, seq // ts),
        in_specs=[
            row(d),
            pl.BlockSpec((1, 6, d), lambda b, s: (b, 0, 0)),
            vec(d),
            pl.BlockSpec((d, IN_PAD), lambda b, s: (0, 0)),
            pl.BlockSpec((A_HEAD_DIM, d), lambda b, s: (0, 0)),
            pl.BlockSpec((CONV_WIDTH, 2 * M_WIDTH), lambda b, s: (0, 0)),
            vec(2 * M_WIDTH), vec(LANES), vec(LANES), vec(LANES), vec(LANES),
            tab, tab, tab, tab, tab, tab,
        ],
        out_specs=[avt_spec if o is None else row(o[0]) for o in outs],
        scratch_shapes=[pltpu.VMEM((ts + 2 * SUBLANES, 2 * M_WIDTH), F32)],
        compiler_params=_params(("arbitrary", "arbitrary")),
        name="inproj",
    )(x, mod, g_pre.reshape(1, d), w, w[:, C_AV:C_AV + A_HEAD_DIM].T, conv_w, conv_b.reshape(1, -1),
      ln_g.reshape(1, -1), ln_b.reshape(1, -1), sm_scale.reshape(1, -1), sm_bias.reshape(1, -1),
      ac, as1, as2, ic, is1, is2)


def _mlstm_kernel(q_ref, k_ref, v_ref, o_ref, gc_ref, gr_ref, gh_ref, h_ref, c_st, n_st, m_st):
    L = q_ref.shape[1]

    @pl.when(pl.program_id(1) == 0)
    def _():
        c_st[...] = jnp.zeros(c_st.shape, F32)
        n_st[...] = jnp.zeros(n_st.shape, F32)
        m_st[...] = jnp.full(m_st.shape, STAB_INIT, F32)

    row_i = lax.broadcasted_iota(I32, (L, L), 0)
    col_i = lax.broadcasted_iota(I32, (L, L), 1)
    tril = row_i >= col_i
    tri_l = jnp.where(tril, 1.0, 0.0).astype(BF16)
    tri_u = jnp.where(row_i <= col_i, 1.0, 0.0).astype(BF16)

    gc = gc_ref[0]
    gr = gr_ref[0, 0]
    lf_c = _log_sigmoid(gc)
    lf_r = _log_sigmoid(gr)
    c_hi, c_lo = _split_bf16(lf_c)
    c_lo2 = (lf_c - c_hi.astype(F32) - c_lo.astype(F32)).astype(BF16)
    b_c = _dot(tri_l, c_hi) + (_dot(tri_l, c_lo) + _dot(tri_l, c_lo2))
    r_hi, r_lo = _split_bf16(lf_r)
    r_lo2 = (lf_r - r_hi.astype(F32) - r_lo.astype(F32)).astype(BF16)
    b_r = _dot(r_hi, tri_u) + (_dot(r_lo, tri_u) + _dot(r_lo2, tri_u))

    for hd in range(M_HEADS):
        sl = slice(hd * M_HEAD_DIM, (hd + 1) * M_HEAD_DIM)
        q = q_ref[0, :, sl]
        k = k_ref[0, :, sl]
        v = v_ref[0, :, sl]
        i_col = gc[:, SM_MI + hd:SM_MI + hd + 1]
        b_col = b_c[:, SM_MF + hd:SM_MF + hd + 1]
        i_row = gr[hd:hd + 1, :]
        b_row = b_r[M_HEADS + hd:M_HEADS + hd + 1, :]
        m_prev = m_st[hd][:, 0:1]
        c_prev = c_st[hd]
        n_prev = n_st[hd]

        log_d = jnp.where(tril, (b_col - b_row) + i_row, -jnp.inf)
        m_inter = b_col + m_prev
        m_t = jnp.maximum(m_inter, jnp.max(log_d, axis=-1, keepdims=True))
        dmat = jnp.exp(log_d - m_t)
        s_qk = _dot_nt(q, k) * dmat
        inter = jnp.exp(m_inter - m_t)
        qf = q.astype(F32)
        num = _dot(s_qk.astype(BF16), v) + inter * _dot_nt(q, c_prev.astype(BF16))
        den = jnp.sum(s_qk, axis=-1, keepdims=True) + inter * jnp.sum(qf * n_prev, axis=-1, keepdims=True)
        hh = num / jnp.maximum(jnp.abs(den), jnp.exp(-m_t))

        b_last = b_col[L - 1:L, :]
        log_w = (b_last - b_col) + i_col
        m_new = jnp.maximum(b_last + m_prev, jnp.max(log_w, axis=0, keepdims=True))
        w = jnp.exp(log_w - m_new)
        decay = jnp.exp((b_last + m_prev) - m_new)
        vw = (v.astype(F32) * w).astype(BF16)
        c_st[hd] = decay * c_prev + lax.dot_general(vw, k, (((0,), (0,)), ((), ())), preferred_element_type=F32)
        n_st[hd] = decay * n_prev + jnp.sum(w * k.astype(F32), axis=0, keepdims=True)
        m_st[hd] = jnp.broadcast_to(m_new, (1, LANES))

        hn = hh * lax.rsqrt(jnp.mean(hh * hh, axis=-1, keepdims=True) + NORM_EPS) * gh_ref[:, sl]
        h_ref[0, :, sl] = (_sigmoid(o_ref[0, :, sl]) * hn).astype(BF16)


def _mlstm(qm, km, vm, om, small, g_head):
    bsz, seq, _ = qm.shape
    L = min(MLSTM_CHUNK, seq)
    nc = seq // L
    gr = small[:, :, SM_MI:SM_MI + 2 * M_HEADS].reshape(bsz, nc, L, 2 * M_HEADS).transpose(0, 1, 3, 2)
    row = lambda width: pl.BlockSpec((1, L, width), lambda b, c: (b, c, 0))
    return pl.pallas_call(
        _mlstm_kernel,
        out_shape=jax.ShapeDtypeStruct((bsz, seq, M_WIDTH), BF16),
        grid=(bsz, nc),
        in_specs=[row(M_WIDTH), row(M_WIDTH), row(M_WIDTH), row(M_WIDTH), row(LANES),
                  pl.BlockSpec((1, 1, 2 * M_HEADS, L), lambda b, c: (b, c, 0, 0)),
                  pl.BlockSpec((1, M_WIDTH), lambda b, c: (0, 0))],
        out_specs=row(M_WIDTH),
        scratch_shapes=[pltpu.VMEM((M_HEADS, M_HEAD_DIM, M_HEAD_DIM), F32),
                        pltpu.VMEM((M_HEADS, 1, M_HEAD_DIM), F32),
                        pltpu.VMEM((M_HEADS, 1, LANES), F32)],
        compiler_params=_params(("arbitrary", "arbitrary")),
        name="mlstm",
    )(qm, km, vm, om, small, gr, g_head.reshape(1, -1))


def _fold_rows(x, op):
    rows, lanes = x.shape
    x = x.reshape(rows // SUBLANES, SUBLANES, lanes)
    assert (rows // SUBLANES) & (rows // SUBLANES - 1) == 0, "row-vreg count must be a power of two"
    while x.shape[0] > 1:
        half = x.shape[0] // 2
        x = op(x[:half], x[half:])
    return x[0]


def _sortable(bits):
    return bits ^ ((bits >> 31) & jnp.int32(0x7FFFFFFF))


_KEY_NEG_INF = int(np.array([-np.inf], np.float32).view(np.int32)[0]) ^ 0x7FFFFFFF
_KEY_NEG_INF = _KEY_NEG_INF - (1 << 32) if _KEY_NEG_INF >= (1 << 31) else _KEY_NEG_INF
_INT_MIN = -(1 << 31)
_NEG_BIG = -1e30


def _dsa_kernel(iq_ref, iwt_ref, aq_ref, ik_ref, ak_ref, avt_ref, o_ref,
                keys_ref, bias_ref, lg_ref, p_ref, acc_ref, a_ref, m_ref, l_ref, *, k_top, kt, seq):
    qb = pl.program_id(1)
    nq = iq_ref.shape[1]
    nkt = (qb * nq) // kt + 1
    qlane = lax.broadcasted_iota(I32, (1, nq), 1)
    lim = qb * nq + (qlane // CHUNK + 1) * CHUNK
    krow = lax.broadcasted_iota(I32, (kt, nq), 0)
    iq = iq_ref[0]
    iwt = iwt_ref[0]

    def score_body(j, carry):
        start = pl.multiple_of(j * kt, kt)
        ikt = ik_ref[0, pl.ds(start, kt), :]
        sc = jnp.zeros((kt, nq), F32)
        for hd in range(IDX_HEADS):
            s = _dot_nt(ikt, iq[:, hd * IDX_DIM:(hd + 1) * IDX_DIM])
            sc = sc + iwt[hd:hd + 1, :] * jnp.maximum(s, 0.0)
        key = _sortable(pltpu.bitcast(sc, I32))
        keys_ref[j] = jnp.where(start + krow < lim, key, _KEY_NEG_INF)
        return carry

    lax.fori_loop(0, nkt, score_body, 0)

    def count(pred):
        def body(j, acc):
            ones = jnp.where(pred(keys_ref[j], j), 1.0, 0.0)
            return acc + _fold_rows(ones, jnp.add)
        part = lax.fori_loop(0, nkt, body, jnp.zeros((SUBLANES, nq), F32))
        return jnp.sum(part, axis=0, keepdims=True)

    kf = float(k_top)
    t0 = jnp.where(count(lambda key, j: key >= 0) >= kf, 0, _INT_MIN).astype(I32)

    def bit_body(i, t):
        cand = t | lax.shift_left(jnp.int32(1), 30 - i)
        return jnp.where(count(lambda key, j: key >= cand) >= kf, cand, t)

    t = lax.fori_loop(0, 31, bit_body, t0)
    n_gt = count(lambda key, j: key > t)
    n_eq = count(lambda key, j: key == t)
    need = kf - n_gt
    tie = (n_eq > need) & (t > _KEY_NEG_INF)

    def tie_search():
        p = jnp.zeros((1, nq), I32)
        for b in reversed(range(max(1, int(seq - 1).bit_length()))):
            cand = p | (1 << b)
            c = count(lambda key, j: (key == t) & (j * kt + krow < cand))
            p = jnp.where(c <= need - 1.0, cand, p)
        return p

    any_tie = jnp.max(jnp.where(tie, 1.0, 0.0)) > 0.0
    p_idx = lax.cond(any_tie, tie_search, lambda: jnp.full((1, nq), seq, I32))
    p_lim = jnp.where(tie, p_idx, jnp.where(t > _KEY_NEG_INF, seq, -1))

    acc_ref[...] = jnp.zeros(acc_ref.shape, F32)
    m_ref[...] = jnp.full(m_ref.shape, _NEG_BIG, F32)
    l_ref[...] = jnp.zeros(l_ref.shape, F32)
    aqs = jnp.concatenate([aq_ref[0, :, hd * A_HEAD_DIM:(hd + 1) * A_HEAD_DIM] for hd in range(A_HEADS)], axis=0)
    scale = A_HEAD_DIM ** -0.5

    def att_body(j, carry):
        start = pl.multiple_of(j * kt, kt)
        key = keys_ref[j]
        sel = (key > t) | ((key == t) & (start + krow <= p_lim))
        bias_ref[...] = jnp.where(sel, 0.0, 3.0 * _NEG_BIG)
        lg_ref[...] = _dot_nt(ak_ref[0, pl.ds(start, kt), :], aqs)
        for hd in range(A_HEADS):
            sl = slice(hd * nq, (hd + 1) * nq)
            lg = lg_ref[:, sl] * scale + bias_ref[...]
            m_old = m_ref[:, sl]
            m_new = jnp.maximum(m_old, jnp.max(_fold_rows(lg, jnp.maximum), axis=0, keepdims=True))
            p = jnp.exp(lg - m_new)
            p_ref[:, sl] = p.astype(BF16)
            alpha = jnp.exp(m_old - m_new)
            a_ref[:, sl] = alpha
            l_ref[:, sl] = alpha * l_ref[:, sl] + jnp.sum(_fold_rows(p, jnp.add), axis=0, keepdims=True)
            m_ref[:, sl] = m_new
        acc_ref[...] = a_ref[...] * acc_ref[...] + _dot(avt_ref[0, j], p_ref[...])
        return carry

    lax.fori_loop(0, nkt, att_body, 0)
    out = acc_ref[...] / l_ref[...]
    for hd in range(A_HEADS):
        o_ref[0, :, hd * A_HEAD_DIM:(hd + 1) * A_HEAD_DIM] = out[:, hd * nq:(hd + 1) * nq].T.astype(BF16)


def _dsa(iq, ik, small, aq, ak, avt):
    bsz, seq, _ = aq.shape
    nq = min(Q_BLOCK, seq)
    kt = avt.shape[3]
    k_top = min(TOPK_KEYS_MAX, seq // 4)
    assert kt >= k_top and kt % nq == 0
    iwt = small[:, :, SM_IW:SM_IW + SUBLANES].transpose(0, 2, 1)
    qrow = lambda width: pl.BlockSpec((1, nq, width), lambda b, q: (b, q, 0))
    full = lambda width: pl.BlockSpec((1, seq, width), lambda b, q: (b, 0, 0))
    return pl.pallas_call(
        functools.partial(_dsa_kernel, k_top=k_top, kt=kt, seq=seq),
        out_shape=jax.ShapeDtypeStruct((bsz, seq, A_WIDTH), BF16),
        grid=(bsz, seq // nq),
        in_specs=[qrow(IDX_HEADS * IDX_DIM),
                  pl.BlockSpec((1, SUBLANES, nq), lambda b, q: (b, 0, q)),
                  qrow(A_WIDTH), full(IDX_DIM), full(A_HEAD_DIM),
                  pl.BlockSpec((1, seq // kt, A_HEAD_DIM, kt), lambda b, q: (b, 0, 0, 0))],
        out_specs=qrow(A_WIDTH),
        scratch_shapes=[pltpu.VMEM((seq // kt, kt, nq), I32),
                        pltpu.VMEM((kt, nq), F32),
                        pltpu.VMEM((kt, A_HEADS * nq), F32),
                        pltpu.VMEM((kt, A_HEADS * nq), BF16),
                        pltpu.VMEM((A_HEAD_DIM, A_HEADS * nq), F32),
                        pltpu.VMEM((1, A_HEADS * nq), F32),
                        pltpu.VMEM((1, A_HEADS * nq), F32),
                        pltpu.VMEM((1, A_HEADS * nq), F32)],
        compiler_params=_params(("arbitrary", "arbitrary")),
        name="dsa",
    )(iq, iwt, aq, ik, ak, avt)


def _pack_bf16_pair(lo, hi):
    lo_b = pltpu.bitcast(lo.astype(BF16).astype(F32), U32)
    hi_b = pltpu.bitcast(hi.astype(BF16).astype(F32), U32)
    return (lo_b >> 16) | (hi_b & jnp.uint32(0xFFFF0000))


def _unpack_bf16_pair(w):
    lo = pltpu.bitcast(w << 16, F32)
    hi = pltpu.bitcast(w & jnp.uint32(0xFFFF0000), F32)
    return lo, hi


def _outproj_kernel(hm_ref, ha_ref, x_ref, mod_ref, gpost_ref, gpre_ref, wo_ref, wr_ref,
                    x1_ref, h2_ref, lt_ref):
    gate = mod_ref[0, 2:3, :]
    shift2 = mod_ref[0, 3:4, :]
    scale2 = mod_ref[0, 4:5, :]
    y = _dot(hm_ref[0], wo_ref[0:M_WIDTH, :]) + _dot(ha_ref[0], wo_ref[M_WIDTH:, :])
    yn = y * lax.rsqrt(jnp.mean(y * y, axis=-1, keepdims=True) + NORM_EPS) * gpost_ref[...]
    x1 = x_ref[0] + gate * yn
    x1_ref[0] = x1
    xn = x1 * lax.rsqrt(jnp.mean(x1 * x1, axis=-1, keepdims=True) + NORM_EPS) * gpre_ref[...]
    h2 = xn * (1.0 + scale2) + shift2
    half = h2.shape[1] // 2
    h2_ref[0] = _pack_bf16_pair(h2[:, :half], h2[:, half:])
    hh, hl = _split_bf16(h2)
    wh, wl = _split_bf16(wr_ref[...])
    lt_ref[0] = _dot_nt(wh, hh) + (_dot_nt(wh, hl) + _dot_nt(wl, hh))


def _outproj(hm, ha, x, mod, g_post, g_ffn_pre, w_out, w_router):
    bsz, seq, d = x.shape
    ts = min(ROW_TILE, seq)
    row = lambda width: pl.BlockSpec((1, ts, width), lambda b, s: (b, s, 0))
    vec = pl.BlockSpec((1, d), lambda b, s: (0, 0))
    x1, h2p, lt = pl.pallas_call(
        _outproj_kernel,
        out_shape=[jax.ShapeDtypeStruct((bsz, seq, d), F32),
                   jax.ShapeDtypeStruct((bsz, seq, d // 2), U32),
                   jax.ShapeDtypeStruct((bsz, N_EXPERTS, seq), F32)],
        grid=(bsz, seq // ts),
        in_specs=[row(M_WIDTH), row(A_WIDTH), row(d),
                  pl.BlockSpec((1, 6, d), lambda b, s: (b, 0, 0)), vec, vec,
                  pl.BlockSpec((d, d), lambda b, s: (0, 0)),
                  pl.BlockSpec((N_EXPERTS, d), lambda b, s: (0, 0))],
        out_specs=[row(d), row(d // 2), pl.BlockSpec((1, N_EXPERTS, ts), lambda b, s: (b, 0, s))],
        compiler_params=_params(("arbitrary", "arbitrary")),
        name="outproj",
    )(hm, ha, x, mod, g_post.reshape(1, d), g_ffn_pre.reshape(1, d), w_out.astype(BF16), w_router.T)
    return x1, h2p, lt


def _route_kernel(lt_ref, br_ref, eid_ref, rnk_ref, gw_ref, cnt_ref, carry):
    first = (pl.program_id(0) == 0) & (pl.program_id(1) == 0)

    @pl.when(first)
    def _():
        carry[...] = jnp.zeros(carry.shape, F32)

    tsr = lt_ref.shape[2]
    scores = _sigmoid(lt_ref[0])
    sel = scores + br_ref[...]
    x3 = sel.reshape(N_GROUPS, GROUP_SIZE, tsr)
    io3 = lax.broadcasted_iota(I32, x3.shape, 1)
    m1 = jnp.max(x3, axis=1, keepdims=True)
    i1 = jnp.min(jnp.where(x3 == m1, io3, GROUP_SIZE), axis=1, keepdims=True)
    m2 = jnp.max(jnp.where(io3 == i1, -jnp.inf, x3), axis=1, keepdims=True)
    gs = (m1 + m2).reshape(N_GROUPS, tsr)
    iog = lax.broadcasted_iota(I32, gs.shape, 0)
    g_rank = jnp.zeros(gs.shape, F32)
    for gp in range(N_GROUPS):
        r = gs[gp:gp + 1, :]
        g_rank = g_rank + jnp.where((r > gs) | ((r == gs) & (iog > gp)), 1.0, 0.0)
    g_sel = g_rank < float(TOPK_GROUPS)
    e_mask = jnp.broadcast_to(g_sel.reshape(N_GROUPS, 1, tsr), x3.shape).reshape(N_EXPERTS, tsr)
    v = jnp.where(e_mask, sel, -jnp.inf)
    ioe = lax.broadcasted_iota(I32, v.shape, 0)
    e_rank = jnp.zeros(v.shape, F32)
    for ep in range(N_EXPERTS):
        r = v[ep:ep + 1, :]
        e_rank = e_rank + jnp.where((r > v) | ((r == v) & (ioe > ep)), 1.0, 0.0)
    e_sel = e_rank < float(TOPK_EXPERTS)
    picked = jnp.where(e_sel, scores, 0.0)
    wd = picked / jnp.sum(picked, axis=0, keepdims=True) * ROUTE_SCALE

    sel_b = jnp.where(e_sel, 1.0, 0.0).astype(BF16)
    r_i = lax.broadcasted_iota(I32, (tsr, tsr), 0)
    c_i = lax.broadcasted_iota(I32, (tsr, tsr), 1)
    incl = _dot(sel_b, jnp.where(r_i <= c_i, 1.0, 0.0).astype(BF16))
    rank = carry[...] + incl - sel_b.astype(F32)
    carry[...] = carry[...] + incl[:, tsr - 1:tsr]
    cnt_ref[...] = jnp.broadcast_to(carry[...], cnt_ref.shape)
    e_r = lax.broadcasted_iota(I32, (N_EXPERTS, N_EXPERTS), 0)
    e_c = lax.broadcasted_iota(I32, (N_EXPERTS, N_EXPERTS), 1)
    slot = _dot(jnp.where(e_c < e_r, 1.0, 0.0).astype(BF16), sel_b)
    ioe_f = ioe.astype(F32)
    for kk in range(TOPK_EXPERTS):
        mk = e_sel & (slot == float(kk))
        eid_ref[0, kk:kk + 1, :] = jnp.sum(jnp.where(mk, ioe_f, 0.0), axis=0, keepdims=True).astype(I32)
        rnk_ref[0, kk:kk + 1, :] = jnp.sum(jnp.where(mk, rank, 0.0), axis=0, keepdims=True).astype(I32)
        gw_ref[0, kk:kk + 1, :] = jnp.sum(jnp.where(mk, wd, 0.0), axis=0, keepdims=True)


def _route(lt, b_router):
    bsz, _, seq = lt.shape
    tsr = min(ROUTE_TILE, seq)
    slab = pl.BlockSpec((1, TOPK_EXPERTS, tsr), lambda b, s: (b, 0, s))
    return pl.pallas_call(
        _route_kernel,
        out_shape=[jax.ShapeDtypeStruct((bsz, TOPK_EXPERTS, seq), I32),
                   jax.ShapeDtypeStruct((bsz, TOPK_EXPERTS, seq), I32),
                   jax.ShapeDtypeStruct((bsz, TOPK_EXPERTS, seq), F32),
                   jax.ShapeDtypeStruct((N_EXPERTS, LANES), F32)],
        grid=(bsz, seq // tsr),
        in_specs=[pl.BlockSpec((1, N_EXPERTS, tsr), lambda b, s: (b, 0, s)),
                  pl.BlockSpec((N_EXPERTS, 1), lambda b, s: (0, 0))],
        out_specs=[slab, slab, slab, pl.BlockSpec((N_EXPERTS, LANES), lambda b, s: (0, 0))],
        scratch_shapes=[pltpu.VMEM((N_EXPERTS, 1), F32)],
        compiler_params=_params(("arbitrary", "arbitrary")),
        name="route",
    )(lt, b_router.reshape(N_EXPERTS, 1))


def _dispatch_kernel(pos_ref, h2_ref, xs_in_ref, xs_ref, sem):
    del xs_in_ref
    tsd = h2_ref.shape[1]

    def row_copy(t, kk):
        return pltpu.make_async_copy(h2_ref.at[0, pl.ds(t, 1)], xs_ref.at[pl.ds(pos_ref[0, kk, t], 1)], sem)

    def body(t, carry):
        for kk in range(TOPK_EXPERTS):
            row_copy(t, kk).start()
        return carry

    lax.fori_loop(0, tsd, body, 0)
    for kk in range(TOPK_EXPERTS):
        pltpu.make_async_copy(h2_ref.at[0], xs_ref.at[pl.ds(0, tsd)], sem).wait()


def _dispatch(pos, h2p, rows):
    bsz, seq, dw = h2p.shape
    tsd = min(DISPATCH_TILE, seq)
    xs0 = jnp.zeros((rows, dw), U32)
    return pl.pallas_call(
        _dispatch_kernel,
        out_shape=jax.ShapeDtypeStruct((rows, dw), U32),
        grid=(bsz, seq // tsd),
        in_specs=[pl.BlockSpec((1, TOPK_EXPERTS, tsd), lambda b, s: (b, 0, s), memory_space=pltpu.SMEM),
                  pl.BlockSpec((1, tsd, dw), lambda b, s: (b, s, 0)),
                  pl.BlockSpec(memory_space=pl.ANY)],
        out_specs=pl.BlockSpec(memory_space=pl.ANY),
        scratch_shapes=[pltpu.SemaphoreType.DMA],
        input_output_aliases={2: 0},
        compiler_params=_params(("arbitrary", "arbitrary"), has_side_effects=True),
        name="dispatch",
    )(pos, h2p, xs0)


def _swiglu_packed(xp, wg_ref, wu_ref, wd_ref):
    lo, hi = _unpack_bf16_pair(xp)
    half = lo.shape[1]
    lo, hi = lo.astype(BF16), hi.astype(BF16)
    g = _dot(lo, wg_ref[0:half, :]) + _dot(hi, wg_ref[half:, :])
    u = _dot(lo, wu_ref[0:half, :]) + _dot(hi, wu_ref[half:, :])
    a = (_silu(g) * u).astype(BF16)
    return _dot(a, wd_ref[...])


def _expert_kernel(be_ref, xs_ref, wg_ref, wu_ref, wd_ref, ys_ref):
    del be_ref
    y = _swiglu_packed(xs_ref[...], wg_ref.at[0], wu_ref.at[0], wd_ref.at[0])
    half = y.shape[1] // 2
    ys_ref[...] = _pack_bf16_pair(y[:, :half], y[:, half:])


def _experts(block_e, xs, w_gate, w_up, w_down):
    rows, dw = xs.shape
    _, d, f = w_gate.shape
    nb = rows // EXPERT_ROWS
    return pl.pallas_call(
        _expert_kernel,
        out_shape=jax.ShapeDtypeStruct((rows, dw), U32),
        grid_spec=pltpu.PrefetchScalarGridSpec(
            num_scalar_prefetch=1,
            grid=(nb,),
            in_specs=[pl.BlockSpec((EXPERT_ROWS, dw), lambda i, be: (i, 0)),
                      pl.BlockSpec((1, d, f), lambda i, be: (be[i], 0, 0)),
                      pl.BlockSpec((1, d, f), lambda i, be: (be[i], 0, 0)),
                      pl.BlockSpec((1, f, d), lambda i, be: (be[i], 0, 0))],
            out_specs=pl.BlockSpec((EXPERT_ROWS, dw), lambda i, be: (i, 0)),
        ),
        compiler_params=_params(("arbitrary",)),
        name="experts",
    )(block_e, xs, w_gate.astype(BF16), w_up.astype(BF16), w_down.astype(BF16))


def _combine_kernel(pos_ref, gw_ref, x1_ref, h2_ref, mod_ref, g_ref, wsg_ref, wsu_ref, wsd_ref, ys_ref,
                    o_ref, buf, sem):
    tsc = h2_ref.shape[1]

    def row_copy(t, kk):
        return pltpu.make_async_copy(ys_ref.at[pl.ds(pos_ref[0, kk, t], 1)], buf.at[kk, pl.ds(t, 1)], sem)

    def body(t, carry):
        for kk in range(TOPK_EXPERTS):
            row_copy(t, kk).start()
        return carry

    lax.fori_loop(0, tsc, body, 0)
    shared = _swiglu_packed(h2_ref[0], wsg_ref, wsu_ref, wsd_ref)
    for kk in range(TOPK_EXPERTS):
        pltpu.make_async_copy(ys_ref.at[pl.ds(0, tsc)], buf.at[kk], sem).wait()

    half = shared.shape[1] // 2
    gw = gw_ref[0]
    y_lo, y_hi = shared[:, :half], shared[:, half:]
    for kk in range(TOPK_EXPERTS):
        lo, hi = _unpack_bf16_pair(buf[kk])
        w = gw[:, kk:kk + 1]
        y_lo = y_lo + w * lo
        y_hi = y_hi + w * hi
    ms = (jnp.sum(y_lo * y_lo, axis=-1, keepdims=True) + jnp.sum(y_hi * y_hi, axis=-1, keepdims=True)) * (0.5 / half)
    inv = lax.rsqrt(ms + NORM_EPS)
    gate = mod_ref[0, 5:6, :]
    g = g_ref[...]
    o_ref[0, :, :half] = x1_ref[0, :, :half] + gate[:, :half] * (y_lo * inv * g[:, :half])
    o_ref[0, :, half:] = x1_ref[0, :, half:] + gate[:, half:] * (y_hi * inv * g[:, half:])


def _combine(pos, gw_t, x1, h2p, mod, g_post, ws_gate, ws_up, ws_down, ys):
    bsz, seq, d = x1.shape
    dw = h2p.shape[2]
    f = ws_gate.shape[1]
    tsc = min(COMBINE_TILE, seq)
    row = lambda width: pl.BlockSpec((1, tsc, width), lambda b, s: (b, s, 0))
    const = lambda shape: pl.BlockSpec(shape, lambda b, s: tuple(0 for _ in shape))
    return pl.pallas_call(
        _combine_kernel,
        out_shape=jax.ShapeDtypeStruct((bsz, seq, d), F32),
        grid=(bsz, seq // tsc),
        in_specs=[pl.BlockSpec((1, TOPK_EXPERTS, tsc), lambda b, s: (b, 0, s), memory_space=pltpu.SMEM),
                  row(TOPK_EXPERTS), row(d), row(dw),
                  pl.BlockSpec((1, 6, d), lambda b, s: (b, 0, 0)),
                  const((1, d)), const((d, f)), const((d, f)), const((f, d)),
                  pl.BlockSpec(memory_space=pl.ANY)],
        out_specs=row(d),
        scratch_shapes=[pltpu.VMEM((TOPK_EXPERTS, tsc, dw), U32), pltpu.SemaphoreType.DMA],
        compiler_params=_params(("arbitrary", "arbitrary")),
        name="combine",
    )(pos, gw_t, x1, h2p, mod, g_post.reshape(1, d), ws_gate.astype(BF16), ws_up.astype(BF16),
      ws_down.astype(BF16), ys)


def _layer(x, c, w_ada, b_ada, g_mix_pre, g_mix_post, g_ffn_pre, g_ffn_post, w_in, conv_w, conv_b, b_igate, b_fgate,
           g_mlstm_head, g_idx_k, b_idx_k, w_out, w_router, b_router, w_exp_gate, w_exp_up, w_exp_down,
           w_sh_gate, w_sh_up, w_sh_down):
    bsz, seq, d = x.shape
    mod = _adaln(c, w_ada, b_ada).reshape(bsz, 6, d)
    qm, km, vm, om, aq, ak, avt, iq, ik, small = _inproj(x, mod, g_mix_pre, w_in, conv_w, conv_b, b_igate, b_fgate,
                                                        g_idx_k, b_idx_k)
    hm = _mlstm(qm, km, vm, om, small, g_mlstm_head)
    ha = _dsa(iq, ik, small, aq, ak, avt)
    x1, h2p, lt = _outproj(hm, ha, x, mod, g_mix_post, g_ffn_pre, w_out, w_router)

    eid, rnk, gw, cnt = _route(lt, b_router)
    counts = cnt[:, 0].astype(I32)
    padded = (counts + EXPERT_ROWS - 1) // EXPERT_ROWS * EXPERT_ROWS
    pad_end = jnp.cumsum(padded)
    pad_start = pad_end - padded
    pos = rnk
    for e in range(N_EXPERTS):
        pos = pos + jnp.where(eid == e, pad_start[e], 0)
    nb = (bsz * seq * TOPK_EXPERTS + EXPERT_ROWS - 1) // EXPERT_ROWS + N_EXPERTS
    block_start = jnp.arange(nb, dtype=I32) * EXPERT_ROWS
    block_e = jnp.minimum(jnp.sum((pad_end[None, :] <= block_start[:, None]).astype(I32), axis=1), N_EXPERTS - 1)

    xs = _dispatch(pos, h2p, nb * EXPERT_ROWS)
    ys = _experts(block_e, xs, w_exp_gate, w_exp_up, w_exp_down)
    return _combine(pos, gw.transpose(0, 2, 1), x1, h2p, mod, g_ffn_post, w_sh_gate, w_sh_up, w_sh_down, ys)


def kernel(x, c, w_ada, b_ada, g_mix_pre, g_mix_post, g_ffn_pre, g_ffn_post, w_in, conv_w, conv_b, b_igate, b_fgate, g_mlstm_head, g_idx_k, b_idx_k, w_out, w_router, b_router, w_exp_gate, w_exp_up, w_exp_down, w_sh_gate, w_sh_up, w_sh_down):
    per_layer = (w_ada, b_ada, g_mix_pre, g_mix_post, g_ffn_pre, g_ffn_post, w_in, conv_w, conv_b, b_igate, b_fgate,
                 g_mlstm_head, g_idx_k, b_idx_k, w_out, w_router, b_router, w_exp_gate, w_exp_up, w_exp_down,
                 w_sh_gate, w_sh_up, w_sh_down)
    for layer in range(w_ada.shape[0]):
        x = _layer(x, c, *(p[layer] for p in per_layer))
    return x
```

```python
import functools

import jax
import jax.numpy as jnp
import numpy as np
from jax import lax
from jax.experimental import pallas as pl
from jax.experimental.pallas import tpu as pltpu

F32 = jnp.float32
BF16 = jnp.bfloat16
I32 = jnp.int32
U32 = jnp.uint32
I16 = jnp.int16

CHUNK = 64
M_HEADS = 4
M_HEAD_DIM = 128
M_WIDTH = M_HEADS * M_HEAD_DIM
CONV_WIDTH = 4
A_HEADS = 4
A_HEAD_DIM = 128
A_WIDTH = A_HEADS * A_HEAD_DIM
IDX_HEADS = 4
IDX_DIM = 64
TOPK_KEYS_MAX = 256
Q_BLOCK = 128
ROPE_THETA = 500000.0
ROPE_FRACTION_DIV = 4
N_EXPERTS = 64
N_GROUPS = 8
GROUP_SIZE = N_EXPERTS // N_GROUPS
TOPK_GROUPS = 4
TOPK_EXPERTS = 8
D_EXPERT = 256
ROUTE_SCALE = 2.5
NORM_EPS = 1e-6
STAB_INIT = -1e30

LANES = 128
SUBLANES = 8
VMEM_LIMIT = 56 * 1024 * 1024

ROW_TILE = 512
MLSTM_CHUNK = 256
ROUTE_TILE = 512
EXPERT_ROWS = 1024
DISPATCH_TILE = 256
COMBINE_TILE = 256
DMA_ISSUE_UNROLL = 4

C_MQ = 0
C_MV = 2 * M_WIDTH
C_MO = 3 * M_WIDTH
C_AQ = 4 * M_WIDTH
C_AK = C_AQ + A_WIDTH
C_AV = C_AK + A_HEAD_DIM
C_IQ = C_AV + A_HEAD_DIM
C_SM = C_IQ + IDX_HEADS * IDX_DIM
IN_PAD = C_SM + LANES
SM_MI = IDX_DIM
SM_MF = IDX_DIM + M_HEADS
SM_IW = IDX_DIM + 2 * M_HEADS


def _params(sem, **kw):
    return pltpu.CompilerParams(dimension_semantics=sem, vmem_limit_bytes=VMEM_LIMIT, **kw)


def _split_bf16(x):
    hi = x.astype(BF16)
    lo = (x - hi.astype(F32)).astype(BF16)
    return hi, lo


def _dot(a, b):
    return jnp.dot(a, b, preferred_element_type=F32)


def _dot_nt(a, b):
    return lax.dot_general(a, b, (((1,), (1,)), ((), ())), preferred_element_type=F32)


def _dot3(a, b):
    ah, al = _split_bf16(a)
    bh, bl = _split_bf16(b)
    return _dot(ah, bh) + (_dot(ah, bl) + _dot(al, bh))


def _sigmoid(x):
    return 1.0 / (1.0 + jnp.exp(-x))


def _silu(x):
    return x * _sigmoid(x)


def _log_sigmoid(x):
    return jnp.minimum(x, 0.0) - jnp.log(1.0 + jnp.exp(-jnp.abs(x)))


def _adaln_kernel(c_ref, w_ref, b_ref, o_ref):
    o_ref[...] = _dot3(_silu(c_ref[...]), w_ref[...]) + b_ref[...]


def _adaln(c, w, b):
    bsz, d = c.shape
    n = w.shape[1]
    tn = d
    return pl.pallas_call(
        _adaln_kernel,
        out_shape=jax.ShapeDtypeStruct((bsz, n), F32),
        grid=(n // tn,),
        in_specs=[
            pl.BlockSpec((bsz, d), lambda j: (0, 0)),
            pl.BlockSpec((d, tn), lambda j: (0, j)),
            pl.BlockSpec((1, tn), lambda j: (0, j)),
        ],
        out_specs=pl.BlockSpec((bsz, tn), lambda j: (0, j)),
        compiler_params=_params(("arbitrary",)),
        name="adaln",
    )(c, w, b.reshape(1, n))


def _rope_tables(seq, head_dim, group):
    rot = head_dim // ROPE_FRACTION_DIV
    half = rot // 2
    inv_freq = jnp.exp(-jnp.log(jnp.float32(ROPE_THETA)) * jnp.arange(half, dtype=F32) * (2.0 / rot))
    ang = jnp.arange(seq, dtype=F32)[:, None] * inv_freq[None, :]
    cos, sin = jnp.cos(ang), jnp.sin(ang)
    pad = group - rot
    c = jnp.concatenate([cos, cos, jnp.ones((seq, pad), F32)], axis=1)
    s1 = jnp.concatenate([-sin, jnp.zeros((seq, half + pad), F32)], axis=1)
    s2 = jnp.concatenate([jnp.zeros((seq, half), F32), sin, jnp.zeros((seq, pad), F32)], axis=1)
    rep = LANES // group
    return jnp.tile(c, (1, rep)), jnp.tile(s1, (1, rep)), jnp.tile(s2, (1, rep)), half


def _rope128(x, c, s1, s2, half):
    back = pltpu.roll(x, LANES - half, 1)
    fwd = pltpu.roll(x, half, 1)
    return x * c + back * s1 + fwd * s2


def _inproj_kernel(x_ref, mod_ref, g_ref, w_ref, wavt_ref, cw_ref, cb_ref, lng_ref, lnb_ref, sms_ref, smb_ref,
                   ac_ref, as1_ref, as2_ref, ic_ref, is1_ref, is2_ref,
                   qm_ref, km_ref, vm_ref, om_ref, aq_ref, ak_ref, avt_ref, iq_ref, ik_ref, sm_ref,
                   ubuf, *, a_half, i_half):
    ts = x_ref.shape[1]
    x = x_ref[0]
    shift = mod_ref[0, 0:1, :]
    scale = mod_ref[0, 1:2, :]
    y = x * lax.rsqrt(jnp.mean(x * x, axis=-1, keepdims=True) + NORM_EPS)
    h = (y * g_ref[...]) * (1.0 + scale) + shift
    hb = h.astype(BF16)

    def proj(lo, width):
        return _dot(hb, w_ref[:, lo:lo + width])

    @pl.when(pl.program_id(1) == 0)
    def _():
        ubuf[0:SUBLANES, :] = jnp.zeros((SUBLANES, 2 * M_WIDTH), F32)

    ubuf[SUBLANES:SUBLANES + ts, :] = proj(C_MQ, 2 * M_WIDTH)
    acc = cb_ref[...] + cw_ref[0:1, :] * ubuf[SUBLANES - 3:SUBLANES - 3 + ts, :]
    for j in range(1, CONV_WIDTH):
        off = SUBLANES - (CONV_WIDTH - 1) + j
        acc = acc + cw_ref[j:j + 1, :] * ubuf[off:off + ts, :]
    ubuf[0:SUBLANES, :] = ubuf[ts:ts + SUBLANES, :]
    qk = _silu(acc)
    qm_ref[0] = qk[:, :M_WIDTH].astype(BF16)
    km_ref[0] = (qk[:, M_WIDTH:] * (M_HEAD_DIM ** -0.5)).astype(BF16)

    vm_ref[0] = proj(C_MV, M_WIDTH).astype(BF16)
    om_ref[0] = proj(C_MO, M_WIDTH)

    ac, as1, as2 = ac_ref[...], as1_ref[...], as2_ref[...]
    aq = proj(C_AQ, A_WIDTH)
    for hd in range(A_HEADS):
        sl = slice(hd * A_HEAD_DIM, (hd + 1) * A_HEAD_DIM)
        aq_ref[0, :, sl] = _rope128(aq[:, sl], ac, as1, as2, a_half).astype(BF16)
    ak_ref[0] = _rope128(proj(C_AK, A_HEAD_DIM), ac, as1, as2, a_half).astype(BF16)
    avt_ref[0, 0] = _dot_nt(wavt_ref[...], hb).astype(BF16)

    ic, is1, is2 = ic_ref[...], is1_ref[...], is2_ref[...]
    iq = proj(C_IQ, IDX_HEADS * IDX_DIM)
    for blk in range(IDX_HEADS * IDX_DIM // LANES):
        sl = slice(blk * LANES, (blk + 1) * LANES)
        iq_ref[0, :, sl] = _rope128(iq[:, sl], ic, is1, is2, i_half).astype(BF16)

    sm = proj(C_SM, LANES)
    sm_ref[0] = sm * sms_ref[...] + smb_ref[...]
    lane = lax.broadcasted_iota(I32, (1, LANES), 1)
    is_k = lane < IDX_DIM
    mu = jnp.sum(jnp.where(is_k, sm, 0.0), axis=-1, keepdims=True) * (1.0 / IDX_DIM)
    dv = jnp.where(is_k, sm - mu, 0.0)
    var = jnp.sum(dv * dv, axis=-1, keepdims=True) * (1.0 / IDX_DIM)
    ikn = dv * lax.rsqrt(var + NORM_EPS) * lng_ref[...] + lnb_ref[...]
    ik_ref[0] = _rope128(ikn, ic, is1, is2, i_half)[:, :IDX_DIM].astype(BF16)


def _inproj(x, mod, g_pre, w_in, conv_w, conv_b, b_igate, b_fgate, g_idx_k, b_idx_k):
    bsz, seq, d = x.shape
    ts = min(ROW_TILE, seq)
    o_mi = 4 * M_WIDTH
    o_aq = o_mi + 2 * M_HEADS
    o_iw = o_aq + A_WIDTH + 2 * A_HEAD_DIM + IDX_HEADS * IDX_DIM + IDX_DIM
    w = jnp.concatenate([w_in[:, :o_mi], w_in[:, o_aq:o_iw], w_in[:, o_mi:o_aq], w_in[:, o_iw:],
                         jnp.zeros((d, IN_PAD - w_in.shape[1]), w_in.dtype)], axis=1).astype(BF16)
    idx_w_scale = (IDX_HEADS ** -0.5) * (IDX_DIM ** -0.5)
    sm_scale = jnp.ones((LANES,), F32).at[SM_IW:SM_IW + IDX_HEADS].set(idx_w_scale)
    sm_bias = jnp.zeros((LANES,), F32).at[SM_MI:SM_MI + M_HEADS].set(b_igate).at[SM_MF:SM_MF + M_HEADS].set(b_fgate)
    ln_g = jnp.zeros((LANES,), F32).at[:IDX_DIM].set(g_idx_k)
    ln_b = jnp.zeros((LANES,), F32).at[:IDX_DIM].set(b_idx_k)
    ac, as1, as2, a_half = _rope_tables(seq, A_HEAD_DIM, A_HEAD_DIM)
    ic, is1, is2, i_half = _rope_tables(seq, IDX_DIM, IDX_DIM)

    row = lambda width: pl.BlockSpec((1, ts, width), lambda b, s: (b, s, 0))
    vec = lambda width: pl.BlockSpec((1, width), lambda b, s: (0, 0))
    tab = pl.BlockSpec((ts, LANES), lambda b, s: (s, 0))
    outs = [(M_WIDTH, BF16), (M_WIDTH, BF16), (M_WIDTH, BF16), (M_WIDTH, F32), (A_WIDTH, BF16),
            (A_HEAD_DIM, BF16), None, (IDX_HEADS * IDX_DIM, BF16), (IDX_DIM, BF16), (LANES, F32)]
    avt_shape = jax.ShapeDtypeStruct((bsz, seq // ts, A_HEAD_DIM, ts), BF16)
    avt_spec = pl.BlockSpec((1, 1, A_HEAD_DIM, ts), lambda b, s: (b, s, 0, 0))
    return pl.pallas_call(
        functools.partial(_inproj_kernel, a_half=a_half, i_half=i_half),
        out_shape=[avt_shape if o is None else jax.ShapeDtypeStruct((bsz, seq, o[0]), o[1]) for o in outs],
        grid=(bsz, seq // ts),
        in_specs=[
            row(d),
            pl.BlockSpec((1, 6, d), lambda b, s: (b, 0, 0)),
            vec(d),
            pl.BlockSpec((d, IN_PAD), lambda b, s: (0, 0)),
            pl.BlockSpec((A_HEAD_DIM, d), lambda b, s: (0, 0)),
            pl.BlockSpec((CONV_WIDTH, 2 * M_WIDTH), lambda b, s: (0, 0)),
            vec(2 * M_WIDTH), vec(LANES), vec(LANES), vec(LANES), vec(LANES),
            tab, tab, tab, tab, tab, tab,
        ],
        out_specs=[avt_spec if o is None else row(o[0]) for o in outs],
        scratch_shapes=[pltpu.VMEM((ts + 2 * SUBLANES, 2 * M_WIDTH), F32)],
        compiler_params=_params(("arbitrary", "arbitrary")),
        name="inproj",
    )(x, mod, g_pre.reshape(1, d), w, w[:, C_AV:C_AV + A_HEAD_DIM].T, conv_w, conv_b.reshape(1, -1),
      ln_g.reshape(1, -1), ln_b.reshape(1, -1), sm_scale.reshape(1, -1), sm_bias.reshape(1, -1),
      ac, as1, as2, ic, is1, is2)


def _mlstm_kernel(q_ref, k_ref, v_ref, o_ref, gc_ref, gr_ref, gh_ref, h_ref, c_st, n_st, m_st):
    L = q_ref.shape[1]

    @pl.when(pl.program_id(1) == 0)
    def _():
        c_st[...] = jnp.zeros(c_st.shape, F32)
        n_st[...] = jnp.zeros(n_st.shape, F32)
        m_st[...] = jnp.full(m_st.shape, STAB_INIT, F32)

    row_i = lax.broadcasted_iota(I32, (L, L), 0)
    col_i = lax.broadcasted_iota(I32, (L, L), 1)
    tril = row_i >= col_i
    tri_l = jnp.where(tril, 1.0, 0.0).astype(BF16)
    tri_u = jnp.where(row_i <= col_i, 1.0, 0.0).astype(BF16)

    gc = gc_ref[0]
    gr = gr_ref[0, 0]
    lf_c = _log_sigmoid(gc)
    lf_r = _log_sigmoid(gr)
    c_hi, c_lo = _split_bf16(lf_c)
    c_lo2 = (lf_c - c_hi.astype(F32) - c_lo.astype(F32)).astype(BF16)
    b_c = _dot(tri_l, c_hi) + (_dot(tri_l, c_lo) + _dot(tri_l, c_lo2))
    r_hi, r_lo = _split_bf16(lf_r)
    r_lo2 = (lf_r - r_hi.astype(F32) - r_lo.astype(F32)).astype(BF16)
    b_r = _dot(r_hi, tri_u) + (_dot(r_lo, tri_u) + _dot(r_lo2, tri_u))

    for hd in range(M_HEADS):
        sl = slice(hd * M_HEAD_DIM, (hd + 1) * M_HEAD_DIM)
        q = q_ref[0, :, sl]
        k = k_ref[0, :, sl]
        v = v_ref[0, :, sl]
        i_col = gc[:, SM_MI + hd:SM_MI + hd + 1]
        b_col = b_c[:, SM_MF + hd:SM_MF + hd + 1]
        i_row = gr[hd:hd + 1, :]
        b_row = b_r[M_HEADS + hd:M_HEADS + hd + 1, :]
        m_prev = m_st[hd][:, 0:1]
        c_prev = c_st[hd]
        n_prev = n_st[hd]

        log_d = jnp.where(tril, (b_col - b_row) + i_row, -jnp.inf)
        m_inter = b_col + m_prev
        m_t = jnp.maximum(m_inter, jnp.max(log_d, axis=-1, keepdims=True))
        dmat = jnp.exp(log_d - m_t)
        s_qk = _dot_nt(q, k) * dmat
        inter = jnp.exp(m_inter - m_t)
        qf = q.astype(F32)
        num = _dot(s_qk.astype(BF16), v) + inter * _dot_nt(q, c_prev.astype(BF16))
        den = jnp.sum(s_qk, axis=-1, keepdims=True) + inter * jnp.sum(qf * n_prev, axis=-1, keepdims=True)
        hh = num / jnp.maximum(jnp.abs(den), jnp.exp(-m_t))

        b_last = b_col[L - 1:L, :]
        log_w = (b_last - b_col) + i_col
        m_new = jnp.maximum(b_last + m_prev, jnp.max(log_w, axis=0, keepdims=True))
        w = jnp.exp(log_w - m_new)
        decay = jnp.exp((b_last + m_prev) - m_new)
        vw = (v.astype(F32) * w).astype(BF16)
        c_st[hd] = decay * c_prev + lax.dot_general(vw, k, (((0,), (0,)), ((), ())), preferred_element_type=F32)
        n_st[hd] = decay * n_prev + jnp.sum(w * k.astype(F32), axis=0, keepdims=True)
        m_st[hd] = jnp.broadcast_to(m_new, (1, LANES))

        hn = hh * lax.rsqrt(jnp.mean(hh * hh, axis=-1, keepdims=True) + NORM_EPS) * gh_ref[:, sl]
        h_ref[0, :, sl] = (_sigmoid(o_ref[0, :, sl]) * hn).astype(BF16)


def _mlstm(qm, km, vm, om, small, g_head):
    bsz, seq, _ = qm.shape
    L = min(MLSTM_CHUNK, seq)
    nc = seq // L
    gr = small[:, :, SM_MI:SM_MI + 2 * M_HEADS].reshape(bsz, nc, L, 2 * M_HEADS).transpose(0, 1, 3, 2)
    row = lambda width: pl.BlockSpec((1, L, width), lambda b, c: (b, c, 0))
    return pl.pallas_call(
        _mlstm_kernel,
        out_shape=jax.ShapeDtypeStruct((bsz, seq, M_WIDTH), BF16),
        grid=(bsz, nc),
        in_specs=[row(M_WIDTH), row(M_WIDTH), row(M_WIDTH), row(M_WIDTH), row(LANES),
                  pl.BlockSpec((1, 1, 2 * M_HEADS, L), lambda b, c: (b, c, 0, 0)),
                  pl.BlockSpec((1, M_WIDTH), lambda b, c: (0, 0))],
        out_specs=row(M_WIDTH),
        scratch_shapes=[pltpu.VMEM((M_HEADS, M_HEAD_DIM, M_HEAD_DIM), F32),
                        pltpu.VMEM((M_HEADS, 1, M_HEAD_DIM), F32),
                        pltpu.VMEM((M_HEADS, 1, LANES), F32)],
        compiler_params=_params(("arbitrary", "arbitrary")),
        name="mlstm",
    )(qm, km, vm, om, small, gr, g_head.reshape(1, -1))


def _fold_rows(x, op, vreg_rows=SUBLANES):
    rows, lanes = x.shape
    x = x.reshape(rows // vreg_rows, vreg_rows, lanes)
    assert (rows // vreg_rows) & (rows // vreg_rows - 1) == 0, "row-vreg count must be a power of two"
    while x.shape[0] > 1:
        half = x.shape[0] // 2
        x = op(x[:half], x[half:])
    return x[0]


def _sortable(bits):
    return bits ^ ((bits >> 31) & jnp.int32(0x7FFFFFFF))


_KEY_NEG_INF = int(np.array([-np.inf], np.float32).view(np.int32)[0]) ^ 0x7FFFFFFF
_KEY_NEG_INF = _KEY_NEG_INF - (1 << 32) if _KEY_NEG_INF >= (1 << 31) else _KEY_NEG_INF
_I16_MIN = -(1 << 15)
_NEG_BIG = -1e30


def _dsa_kernel(iq_ref, iwt_ref, aq_ref, ik_ref, ak_ref, avt_ref, o_ref,
                keys_ref, hi_ref, lo_ref, tk_ref, bias_ref, lg_ref, p_ref, acc_ref, a_ref, m_ref, l_ref, *, k_top, kt, seq):
    qb = pl.program_id(1)
    nq = iq_ref.shape[1]
    nkt = (qb * nq) // kt + 1
    qlane = lax.broadcasted_iota(I32, (1, nq), 1)
    lim = qb * nq + (qlane // CHUNK + 1) * CHUNK
    krow = lax.broadcasted_iota(I32, (kt, nq), 0)
    iwt = iwt_ref[0]
    iqs = jnp.concatenate([iq_ref[0, :, hd * IDX_DIM:(hd + 1) * IDX_DIM] for hd in range(IDX_HEADS)], axis=0)

    def score_body(j, carry):
        start = pl.multiple_of(j * kt, kt)
        lg_ref[...] = _dot_nt(ik_ref[0, pl.ds(start, kt), :], iqs)
        sc = jnp.zeros((kt, nq), F32)
        for hd in range(IDX_HEADS):
            sc = sc + iwt[hd:hd + 1, :] * jnp.maximum(lg_ref[:, hd * nq:(hd + 1) * nq], 0.0)
        key = _sortable(pltpu.bitcast(sc, I32))
        key = jnp.where(start + krow < lim, key, _KEY_NEG_INF)
        keys_ref[j] = key
        hi_ref[j] = (key >> 16).astype(I16)
        return carry

    lax.fori_loop(0, nkt, score_body, 0)

    def count(ref, pred):
        one, zero = jnp.int16(1), jnp.int16(0)

        def body(j, acc):
            return acc + _fold_rows(jnp.where(pred(ref[j]), one, zero), jnp.add, 2 * SUBLANES)
        part = lax.fori_loop(0, nkt, body, jnp.zeros((2 * SUBLANES, nq), I16))
        return jnp.sum(part.astype(I32), axis=0, keepdims=True)

    def kth_largest_16(ref, k):
        v = jnp.where(count(ref, lambda x: x >= jnp.int16(0)) >= k, 0, _I16_MIN).astype(I32)
        for b in reversed(range(15)):
            cand = v | (1 << b)
            cand16 = cand.astype(I16)
            v = jnp.where(count(ref, lambda x: x >= cand16) >= k, cand, v)
        return v

    t_hi = kth_largest_16(hi_ref, k_top)
    t_hi16 = t_hi.astype(I16)

    def low_fill(j, carry):
        key = keys_ref[j]
        lo = (key & 0xFFFF) + _I16_MIN
        lo_ref[j] = jnp.where((key >> 16) == t_hi, lo, _I16_MIN).astype(I16)
        return carry

    lax.fori_loop(0, nkt, low_fill, 0)
    n_hi = count(hi_ref, lambda x: x > t_hi16)
    t_lo = kth_largest_16(lo_ref, k_top - n_hi)
    t_lo16 = t_lo.astype(I16)
    t = (t_hi << 16) | (t_lo - _I16_MIN)
    n_gt = n_hi + count(lo_ref, lambda x: x > t_lo16)
    n_eq = count(lo_ref, lambda x: x == t_lo16)
    need = k_top - n_gt
    tie = (n_eq > need) & (t > _KEY_NEG_INF)

    def tie_search():
        def fill(j, carry):
            tk_ref[j] = jnp.where(keys_ref[j] == t, (seq - 1) - (j * kt + krow), -1).astype(I16)
            return carry

        lax.fori_loop(0, nkt, fill, 0)
        u = jnp.zeros((1, nq), I32)
        for b in reversed(range(max(1, int(seq - 1).bit_length()))):
            cand = u | (1 << b)
            cand16 = cand.astype(I16)
            u = jnp.where(count(tk_ref, lambda x: x >= cand16) >= need, cand, u)
        return (seq - 1) - u

    any_tie = jnp.max(jnp.where(tie, 1.0, 0.0)) > 0.0
    p_idx = lax.cond(any_tie, tie_search, lambda: jnp.full((1, nq), seq, I32))
    p_lim = jnp.where(tie, p_idx, jnp.where(t > _KEY_NEG_INF, seq, -1))

    acc_ref[...] = jnp.zeros(acc_ref.shape, F32)
    m_ref[...] = jnp.full(m_ref.shape, _NEG_BIG, F32)
    l_ref[...] = jnp.zeros(l_ref.shape, F32)
    aqs = jnp.concatenate([aq_ref[0, :, hd * A_HEAD_DIM:(hd + 1) * A_HEAD_DIM] for hd in range(A_HEADS)], axis=0)
    scale = A_HEAD_DIM ** -0.5

    def att_body(j, carry):
        start = pl.multiple_of(j * kt, kt)
        key = keys_ref[j]
        sel = (key > t) | ((key == t) & (start + krow <= p_lim))
        bias_ref[...] = jnp.where(sel, 0.0, 3.0 * _NEG_BIG)
        lg_ref[...] = _dot_nt(ak_ref[0, pl.ds(start, kt), :], aqs)
        for hd in range(A_HEADS):
            sl = slice(hd * nq, (hd + 1) * nq)
            lg = lg_ref[:, sl] * scale + bias_ref[...]
            m_old = m_ref[:, sl]
            m_new = jnp.maximum(m_old, jnp.max(_fold_rows(lg, jnp.maximum), axis=0, keepdims=True))
            p = jnp.exp(lg - m_new)
            p_ref[:, sl] = p.astype(BF16)
            alpha = jnp.exp(m_old - m_new)
            a_ref[:, sl] = alpha
            l_ref[:, sl] = alpha * l_ref[:, sl] + jnp.sum(_fold_rows(p, jnp.add), axis=0, keepdims=True)
            m_ref[:, sl] = m_new
        acc_ref[...] = a_ref[...] * acc_ref[...] + _dot(avt_ref[0, j], p_ref[...])
        return carry

    lax.fori_loop(0, nkt, att_body, 0)
    out = acc_ref[...] / l_ref[...]
    for hd in range(A_HEADS):
        o_ref[0, :, hd * A_HEAD_DIM:(hd + 1) * A_HEAD_DIM] = out[:, hd * nq:(hd + 1) * nq].T.astype(BF16)


def _dsa(iq, ik, small, aq, ak, avt):
    bsz, seq, _ = aq.shape
    nq = min(Q_BLOCK, seq)
    kt = avt.shape[3]
    k_top = min(TOPK_KEYS_MAX, seq // 4)
    assert kt >= k_top and kt % nq == 0
    iwt = small[:, :, SM_IW:SM_IW + SUBLANES].transpose(0, 2, 1)
    qrow = lambda width: pl.BlockSpec((1, nq, width), lambda b, q: (b, q, 0))
    full = lambda width: pl.BlockSpec((1, seq, width), lambda b, q: (b, 0, 0))
    return pl.pallas_call(
        functools.partial(_dsa_kernel, k_top=k_top, kt=kt, seq=seq),
        out_shape=jax.ShapeDtypeStruct((bsz, seq, A_WIDTH), BF16),
        grid=(bsz, seq // nq),
        in_specs=[qrow(IDX_HEADS * IDX_DIM),
                  pl.BlockSpec((1, SUBLANES, nq), lambda b, q: (b, 0, q)),
                  qrow(A_WIDTH), full(IDX_DIM), full(A_HEAD_DIM),
                  pl.BlockSpec((1, seq // kt, A_HEAD_DIM, kt), lambda b, q: (b, 0, 0, 0))],
        out_specs=qrow(A_WIDTH),
        scratch_shapes=[pltpu.VMEM((seq // kt, kt, nq), I32),
                        pltpu.VMEM((seq // kt, kt, nq), I16),
                        pltpu.VMEM((seq // kt, kt, nq), I16),
                        pltpu.VMEM((seq // kt, kt, nq), I16),
                        pltpu.VMEM((kt, nq), F32),
                        pltpu.VMEM((kt, A_HEADS * nq), F32),
                        pltpu.VMEM((kt, A_HEADS * nq), BF16),
                        pltpu.VMEM((A_HEAD_DIM, A_HEADS * nq), F32),
                        pltpu.VMEM((1, A_HEADS * nq), F32),
                        pltpu.VMEM((1, A_HEADS * nq), F32),
                        pltpu.VMEM((1, A_HEADS * nq), F32)],
        compiler_params=_params(("arbitrary", "arbitrary")),
        name="dsa",
    )(iq, iwt, aq, ik, ak, avt)


def _pack_bf16_pair(lo, hi):
    lo_b = pltpu.bitcast(lo.astype(BF16).astype(F32), U32)
    hi_b = pltpu.bitcast(hi.astype(BF16).astype(F32), U32)
    return (lo_b >> 16) | (hi_b & jnp.uint32(0xFFFF0000))


def _unpack_bf16_pair(w):
    lo = pltpu.bitcast(w << 16, F32)
    hi = pltpu.bitcast(w & jnp.uint32(0xFFFF0000), F32)
    return lo, hi


def _outproj_kernel(hm_ref, ha_ref, x_ref, mod_ref, gpost_ref, gpre_ref, wo_ref, wr_ref,
                    x1_ref, h2_ref, lt_ref):
    gate = mod_ref[0, 2:3, :]
    shift2 = mod_ref[0, 3:4, :]
    scale2 = mod_ref[0, 4:5, :]
    y = _dot(hm_ref[0], wo_ref[0:M_WIDTH, :]) + _dot(ha_ref[0], wo_ref[M_WIDTH:, :])
    yn = y * lax.rsqrt(jnp.mean(y * y, axis=-1, keepdims=True) + NORM_EPS) * gpost_ref[...]
    x1 = x_ref[0] + gate * yn
    x1_ref[0] = x1
    xn = x1 * lax.rsqrt(jnp.mean(x1 * x1, axis=-1, keepdims=True) + NORM_EPS) * gpre_ref[...]
    h2 = xn * (1.0 + scale2) + shift2
    half = h2.shape[1] // 2
    h2_ref[0] = _pack_bf16_pair(h2[:, :half], h2[:, half:])
    hh, hl = _split_bf16(h2)
    wh, wl = _split_bf16(wr_ref[...])
    lt_ref[0] = _dot_nt(wh, hh) + (_dot_nt(wh, hl) + _dot_nt(wl, hh))


def _outproj(hm, ha, x, mod, g_post, g_ffn_pre, w_out, w_router):
    bsz, seq, d = x.shape
    ts = min(ROW_TILE, seq)
    row = lambda width: pl.BlockSpec((1, ts, width), lambda b, s: (b, s, 0))
    vec = pl.BlockSpec((1, d), lambda b, s: (0, 0))
    x1, h2p, lt = pl.pallas_call(
        _outproj_kernel,
        out_shape=[jax.ShapeDtypeStruct((bsz, seq, d), F32),
                   jax.ShapeDtypeStruct((bsz, seq, d // 2), U32),
                   jax.ShapeDtypeStruct((bsz, N_EXPERTS, seq), F32)],
        grid=(bsz, seq // ts),
        in_specs=[row(M_WIDTH), row(A_WIDTH), row(d),
                  pl.BlockSpec((1, 6, d), lambda b, s: (b, 0, 0)), vec, vec,
                  pl.BlockSpec((d, d), lambda b, s: (0, 0)),
                  pl.BlockSpec((N_EXPERTS, d), lambda b, s: (0, 0))],
        out_specs=[row(d), row(d // 2), pl.BlockSpec((1, N_EXPERTS, ts), lambda b, s: (b, 0, s))],
        compiler_params=_params(("arbitrary", "arbitrary")),
        name="outproj",
    )(hm, ha, x, mod, g_post.reshape(1, d), g_ffn_pre.reshape(1, d), w_out.astype(BF16), w_router.T)
    return x1, h2p, lt


def _route_kernel(lt_ref, br_ref, eid_ref, rnk_ref, gw_ref, cnt_ref, carry):
    first = (pl.program_id(0) == 0) & (pl.program_id(1) == 0)

    @pl.when(first)
    def _():
        carry[...] = jnp.zeros(carry.shape, F32)

    tsr = lt_ref.shape[2]
    scores = _sigmoid(lt_ref[0])
    sel = scores + br_ref[...]
    x3 = sel.reshape(N_GROUPS, GROUP_SIZE, tsr)
    io3 = lax.broadcasted_iota(I32, x3.shape, 1)
    m1 = jnp.max(x3, axis=1, keepdims=True)
    i1 = jnp.min(jnp.where(x3 == m1, io3, GROUP_SIZE), axis=1, keepdims=True)
    m2 = jnp.max(jnp.where(io3 == i1, -jnp.inf, x3), axis=1, keepdims=True)
    gs = (m1 + m2).reshape(N_GROUPS, tsr)
    iog = lax.broadcasted_iota(I32, gs.shape, 0)
    g_rank = jnp.zeros(gs.shape, F32)
    for gp in range(N_GROUPS):
        r = gs[gp:gp + 1, :]
        g_rank = g_rank + jnp.where((r > gs) | ((r == gs) & (iog > gp)), 1.0, 0.0)
    g_sel = g_rank < float(TOPK_GROUPS)
    e_mask = jnp.broadcast_to(g_sel.reshape(N_GROUPS, 1, tsr), x3.shape).reshape(N_EXPERTS, tsr)
    v = jnp.where(e_mask, sel, -jnp.inf)
    ioe = lax.broadcasted_iota(I32, v.shape, 0)
    e_rank = jnp.zeros(v.shape, F32)
    for ep in range(N_EXPERTS):
        r = v[ep:ep + 1, :]
        e_rank = e_rank + jnp.where((r > v) | ((r == v) & (ioe > ep)), 1.0, 0.0)
    e_sel = e_rank < float(TOPK_EXPERTS)
    picked = jnp.where(e_sel, scores, 0.0)
    wd = picked / jnp.sum(picked, axis=0, keepdims=True) * ROUTE_SCALE

    sel_b = jnp.where(e_sel, 1.0, 0.0).astype(BF16)
    r_i = lax.broadcasted_iota(I32, (tsr, tsr), 0)
    c_i = lax.broadcasted_iota(I32, (tsr, tsr), 1)
    incl = _dot(sel_b, jnp.where(r_i <= c_i, 1.0, 0.0).astype(BF16))
    rank = carry[...] + incl - sel_b.astype(F32)
    carry[...] = carry[...] + incl[:, tsr - 1:tsr]
    cnt_ref[...] = jnp.broadcast_to(carry[...], cnt_ref.shape)
    e_r = lax.broadcasted_iota(I32, (N_EXPERTS, N_EXPERTS), 0)
    e_c = lax.broadcasted_iota(I32, (N_EXPERTS, N_EXPERTS), 1)
    slot = _dot(jnp.where(e_c < e_r, 1.0, 0.0).astype(BF16), sel_b)
    ioe_f = ioe.astype(F32)
    for kk in range(TOPK_EXPERTS):
        mk = e_sel & (slot == float(kk))
        eid_ref[0, kk:kk + 1, :] = jnp.sum(jnp.where(mk, ioe_f, 0.0), axis=0, keepdims=True).astype(I32)
        rnk_ref[0, kk:kk + 1, :] = jnp.sum(jnp.where(mk, rank, 0.0), axis=0, keepdims=True).astype(I32)
        gw_ref[0, kk:kk + 1, :] = jnp.sum(jnp.where(mk, wd, 0.0), axis=0, keepdims=True)


def _route(lt, b_router):
    bsz, _, seq = lt.shape
    tsr = min(ROUTE_TILE, seq)
    slab = pl.BlockSpec((1, TOPK_EXPERTS, tsr), lambda b, s: (b, 0, s))
    return pl.pallas_call(
        _route_kernel,
        out_shape=[jax.ShapeDtypeStruct((bsz, TOPK_EXPERTS, seq), I32),
                   jax.ShapeDtypeStruct((bsz, TOPK_EXPERTS, seq), I32),
                   jax.ShapeDtypeStruct((bsz, TOPK_EXPERTS, seq), F32),
                   jax.ShapeDtypeStruct((N_EXPERTS, LANES), F32)],
        grid=(bsz, seq // tsr),
        in_specs=[pl.BlockSpec((1, N_EXPERTS, tsr), lambda b, s: (b, 0, s)),
                  pl.BlockSpec((N_EXPERTS, 1), lambda b, s: (0, 0))],
        out_specs=[slab, slab, slab, pl.BlockSpec((N_EXPERTS, LANES), lambda b, s: (0, 0))],
        scratch_shapes=[pltpu.VMEM((N_EXPERTS, 1), F32)],
        compiler_params=_params(("arbitrary", "arbitrary")),
        name="route",
    )(lt, b_router.reshape(N_EXPERTS, 1))


def _dispatch_kernel(pos_ref, h2_ref, xs_in_ref, xs_ref, sem):
    del xs_in_ref
    tsd = h2_ref.shape[1]

    def row_copy(t, kk):
        return pltpu.make_async_copy(h2_ref.at[0, pl.ds(t, 1)], xs_ref.at[pl.ds(pos_ref[0, kk, t], 1)], sem)

    def body(t, carry):
        for kk in range(TOPK_EXPERTS):
            row_copy(t, kk).start()
        return carry

    lax.fori_loop(0, tsd, body, 0, unroll=DMA_ISSUE_UNROLL)
    for kk in range(TOPK_EXPERTS):
        pltpu.make_async_copy(h2_ref.at[0], xs_ref.at[pl.ds(0, tsd)], sem).wait()


def _dispatch(pos, h2p, rows):
    bsz, seq, dw = h2p.shape
    tsd = min(DISPATCH_TILE, seq)
    xs0 = jnp.zeros((rows, dw), U32)
    return pl.pallas_call(
        _dispatch_kernel,
        out_shape=jax.ShapeDtypeStruct((rows, dw), U32),
        grid=(bsz, seq // tsd),
        in_specs=[pl.BlockSpec((1, TOPK_EXPERTS, tsd), lambda b, s: (b, 0, s), memory_space=pltpu.SMEM),
                  pl.BlockSpec((1, tsd, dw), lambda b, s: (b, s, 0)),
                  pl.BlockSpec(memory_space=pl.ANY)],
        out_specs=pl.BlockSpec(memory_space=pl.ANY),
        scratch_shapes=[pltpu.SemaphoreType.DMA],
        input_output_aliases={2: 0},
        compiler_params=_params(("arbitrary", "arbitrary"), has_side_effects=True),
        name="dispatch",
    )(pos, h2p, xs0)


def _swiglu_packed(xp, wg_ref, wu_ref, wd_ref):
    lo, hi = _unpack_bf16_pair(xp)
    half = lo.shape[1]
    lo, hi = lo.astype(BF16), hi.astype(BF16)
    g = _dot(lo, wg_ref[0:half, :]) + _dot(hi, wg_ref[half:, :])
    u = _dot(lo, wu_ref[0:half, :]) + _dot(hi, wu_ref[half:, :])
    a = (_silu(g) * u).astype(BF16)
    return _dot(a, wd_ref[...])


def _expert_kernel(be_ref, xs_ref, wg_ref, wu_ref, wd_ref, ys_ref):
    del be_ref
    y = _swiglu_packed(xs_ref[...], wg_ref.at[0], wu_ref.at[0], wd_ref.at[0])
    half = y.shape[1] // 2
    ys_ref[...] = _pack_bf16_pair(y[:, :half], y[:, half:])


def _experts(block_e, xs, w_gate, w_up, w_down):
    rows, dw = xs.shape
    _, d, f = w_gate.shape
    nb = rows // EXPERT_ROWS
    return pl.pallas_call(
        _expert_kernel,
        out_shape=jax.ShapeDtypeStruct((rows, dw), U32),
        grid_spec=pltpu.PrefetchScalarGridSpec(
            num_scalar_prefetch=1,
            grid=(nb,),
            in_specs=[pl.BlockSpec((EXPERT_ROWS, dw), lambda i, be: (i, 0)),
                      pl.BlockSpec((1, d, f), lambda i, be: (be[i], 0, 0)),
                      pl.BlockSpec((1, d, f), lambda i, be: (be[i], 0, 0)),
                      pl.BlockSpec((1, f, d), lambda i, be: (be[i], 0, 0))],
            out_specs=pl.BlockSpec((EXPERT_ROWS, dw), lambda i, be: (i, 0)),
        ),
        compiler_params=_params(("arbitrary",)),
        name="experts",
    )(block_e, xs, w_gate.astype(BF16), w_up.astype(BF16), w_down.astype(BF16))


def _combine_kernel(pos_ref, gw_ref, x1_ref, h2_ref, mod_ref, g_ref, wsg_ref, wsu_ref, wsd_ref, ys_ref,
                    o_ref, buf, sem):
    tsc = h2_ref.shape[1]

    def row_copy(t, kk):
        return pltpu.make_async_copy(ys_ref.at[pl.ds(pos_ref[0, kk, t], 1)], buf.at[kk, pl.ds(t, 1)], sem)

    def body(t, carry):
        for kk in range(TOPK_EXPERTS):
            row_copy(t, kk).start()
        return carry

    lax.fori_loop(0, tsc, body, 0, unroll=DMA_ISSUE_UNROLL)
    shared = _swiglu_packed(h2_ref[0], wsg_ref, wsu_ref, wsd_ref)
    for kk in range(TOPK_EXPERTS):
        pltpu.make_async_copy(ys_ref.at[pl.ds(0, tsc)], buf.at[kk], sem).wait()

    half = shared.shape[1] // 2
    gw = gw_ref[0]
    y_lo, y_hi = shared[:, :half], shared[:, half:]
    for kk in range(TOPK_EXPERTS):
        lo, hi = _unpack_bf16_pair(buf[kk])
        w = gw[:, kk:kk + 1]
        y_lo = y_lo + w * lo
        y_hi = y_hi + w * hi
    ms = (jnp.sum(y_lo * y_lo, axis=-1, keepdims=True) + jnp.sum(y_hi * y_hi, axis=-1, keepdims=True)) * (0.5 / half)
    inv = lax.rsqrt(ms + NORM_EPS)
    gate = mod_ref[0, 5:6, :]
    g = g_ref[...]
    o_ref[0, :, :half] = x1_ref[0, :, :half] + gate[:, :half] * (y_lo * inv * g[:, :half])
    o_ref[0, :, half:] = x1_ref[0, :, half:] + gate[:, half:] * (y_hi * inv * g[:, half:])


def _combine(pos, gw_t, x1, h2p, mod, g_post, ws_gate, ws_up, ws_down, ys):
    bsz, seq, d = x1.shape
    dw = h2p.shape[2]
    f = ws_gate.shape[1]
    tsc = min(COMBINE_TILE, seq)
    row = lambda width: pl.BlockSpec((1, tsc, width), lambda b, s: (b, s, 0))
    const = lambda shape: pl.BlockSpec(shape, lambda b, s: tuple(0 for _ in shape))
    return pl.pallas_call(
        _combine_kernel,
        out_shape=jax.ShapeDtypeStruct((bsz, seq, d), F32),
        grid=(bsz, seq // tsc),
        in_specs=[pl.BlockSpec((1, TOPK_EXPERTS, tsc), lambda b, s: (b, 0, s), memory_space=pltpu.SMEM),
                  row(TOPK_EXPERTS), row(d), row(dw),
                  pl.BlockSpec((1, 6, d), lambda b, s: (b, 0, 0)),
                  const((1, d)), const((d, f)), const((d, f)), const((f, d)),
                  pl.BlockSpec(memory_space=pl.ANY)],
        out_specs=row(d),
        scratch_shapes=[pltpu.VMEM((TOPK_EXPERTS, tsc, dw), U32), pltpu.SemaphoreType.DMA],
        compiler_params=_params(("arbitrary", "arbitrary")),
        name="combine",
    )(pos, gw_t, x1, h2p, mod, g_post.reshape(1, d), ws_gate.astype(BF16), ws_up.astype(BF16),
      ws_down.astype(BF16), ys)


def _layer(x, c, w_ada, b_ada, g_mix_pre, g_mix_post, g_ffn_pre, g_ffn_post, w_in, conv_w, conv_b, b_igate, b_fgate,
           g_mlstm_head, g_idx_k, b_idx_k, w_out, w_router, b_router, w_exp_gate, w_exp_up, w_exp_down,
           w_sh_gate, w_sh_up, w_sh_down):
    bsz, seq, d = x.shape
    mod = _adaln(c, w_ada, b_ada).reshape(bsz, 6, d)
    qm, km, vm, om, aq, ak, avt, iq, ik, small = _inproj(x, mod, g_mix_pre, w_in, conv_w, conv_b, b_igate, b_fgate,
                                                        g_idx_k, b_idx_k)
    hm = _mlstm(qm, km, vm, om, small, g_mlstm_head)
    ha = _dsa(iq, ik, small, aq, ak, avt)
    x1, h2p, lt = _outproj(hm, ha, x, mod, g_mix_post, g_ffn_pre, w_out, w_router)

    eid, rnk, gw, cnt = _route(lt, b_router)
    counts = cnt[:, 0].astype(I32)
    padded = (counts + EXPERT_ROWS - 1) // EXPERT_ROWS * EXPERT_ROWS
    pad_end = jnp.cumsum(padded)
    pad_start = pad_end - padded
    pos = rnk
    for e in range(N_EXPERTS):
        pos = pos + jnp.where(eid == e, pad_start[e], 0)
    nb = (bsz * seq * TOPK_EXPERTS + EXPERT_ROWS - 1) // EXPERT_ROWS + N_EXPERTS
    block_start = jnp.arange(nb, dtype=I32) * EXPERT_ROWS
    block_e = jnp.minimum(jnp.sum((pad_end[None, :] <= block_start[:, None]).astype(I32), axis=1), N_EXPERTS - 1)

    xs = _dispatch(pos, h2p, nb * EXPERT_ROWS)
    ys = _experts(block_e, xs, w_exp_gate, w_exp_up, w_exp_down)
    return _combine(pos, gw.transpose(0, 2, 1), x1, h2p, mod, g_ffn_post, w_sh_gate, w_sh_up, w_sh_down, ys)


def kernel(x, c, w_ada, b_ada, g_mix_pre, g_mix_post, g_ffn_pre, g_ffn_post, w_in, conv_w, conv_b, b_igate, b_fgate, g_mlstm_head, g_idx_k, b_idx_k, w_out, w_router, b_router, w_exp_gate, w_exp_up, w_exp_down, w_sh_gate, w_sh_up, w_sh_down):
    per_layer = (w_ada, b_ada, g_mix_pre, g_mix_post, g_ffn_pre, g_ffn_post, w_in, conv_w, conv_b, b_igate, b_fgate,
                 g_mlstm_head, g_idx_k, b_idx_k, w_out, w_router, b_router, w_exp_gate, w_exp_up, w_exp_down,
                 w_sh_gate, w_sh_up, w_sh_down)
    for layer in range(w_ada.shape[0]):
        x = _layer(x, c, *(p[layer] for p in per_layer))
    return x
```

```python
import functools

import jax
import jax.numpy as jnp
import numpy as np
from jax import lax
from jax.experimental import pallas as pl
from jax.experimental.pallas import tpu as pltpu

F32 = jnp.float32
BF16 = jnp.bfloat16
I32 = jnp.int32
U32 = jnp.uint32

CHUNK = 64
M_HEADS = 4
M_HEAD_DIM = 128
M_WIDTH = M_HEADS * M_HEAD_DIM
CONV_WIDTH = 4
A_HEADS = 4
A_HEAD_DIM = 128
A_WIDTH = A_HEADS * A_HEAD_DIM
IDX_HEADS = 4
IDX_DIM = 64
TOPK_KEYS_MAX = 256
Q_BLOCK = 128
ROPE_THETA = 500000.0
ROPE_FRACTION_DIV = 4
N_EXPERTS = 64
N_GROUPS = 8
GROUP_SIZE = N_EXPERTS // N_GROUPS
TOPK_GROUPS = 4
TOPK_EXPERTS = 8
D_EXPERT = 256
ROUTE_SCALE = 2.5
NORM_EPS = 1e-6
STAB_INIT = -1e30

LANES = 128
SUBLANES = 8
VMEM_LIMIT = 56 * 1024 * 1024

ROW_TILE = 512
MLSTM_CHUNK = 256
ROUTE_TILE = 512
EXPERT_ROWS = 1024
DISPATCH_TILE = 256
COMBINE_TILE = 256
DMA_ISSUE_UNROLL = 4

C_MQ = 0
C_MV = 2 * M_WIDTH
C_MO = 3 * M_WIDTH
C_AQ = 4 * M_WIDTH
C_AK = C_AQ + A_WIDTH
C_AV = C_AK + A_HEAD_DIM
C_IQ = C_AV + A_HEAD_DIM
C_SM = C_IQ + IDX_HEADS * IDX_DIM
IN_PAD = C_SM + LANES
SM_MI = IDX_DIM
SM_MF = IDX_DIM + M_HEADS
SM_IW = IDX_DIM + 2 * M_HEADS


def _params(sem, **kw):
    return pltpu.CompilerParams(dimension_semantics=sem, vmem_limit_bytes=VMEM_LIMIT, **kw)


def _split_bf16(x):
    hi = x.astype(BF16)
    lo = (x - hi.astype(F32)).astype(BF16)
    return hi, lo


def _dot(a, b):
    return jnp.dot(a, b, preferred_element_type=F32)


def _dot_nt(a, b):
    return lax.dot_general(a, b, (((1,), (1,)), ((), ())), preferred_element_type=F32)


def _dot3(a, b):
    ah, al = _split_bf16(a)
    bh, bl = _split_bf16(b)
    return _dot(ah, bh) + (_dot(ah, bl) + _dot(al, bh))


def _sigmoid(x):
    return 1.0 / (1.0 + jnp.exp(-x))


def _silu(x):
    return x * _sigmoid(x)


def _log_sigmoid(x):
    return jnp.minimum(x, 0.0) - jnp.log(1.0 + jnp.exp(-jnp.abs(x)))


def _adaln_kernel(c_ref, w_ref, b_ref, o_ref):
    o_ref[...] = _dot3(_silu(c_ref[...]), w_ref[...]) + b_ref[...]


def _adaln(c, w, b):
    bsz, d = c.shape
    n = w.shape[1]
    tn = d
    return pl.pallas_call(
        _adaln_kernel,
        out_shape=jax.ShapeDtypeStruct((bsz, n), F32),
        grid=(n // tn,),
        in_specs=[
            pl.BlockSpec((bsz, d), lambda j: (0, 0)),
            pl.BlockSpec((d, tn), lambda j: (0, j)),
            pl.BlockSpec((1, tn), lambda j: (0, j)),
        ],
        out_specs=pl.BlockSpec((bsz, tn), lambda j: (0, j)),
        compiler_params=_params(("arbitrary",)),
        name="adaln",
    )(c, w, b.reshape(1, n))


def _rope_tables(seq, head_dim, group):
    rot = head_dim // ROPE_FRACTION_DIV
    half = rot // 2
    inv_freq = jnp.exp(-jnp.log(jnp.float32(ROPE_THETA)) * jnp.arange(half, dtype=F32) * (2.0 / rot))
    ang = jnp.arange(seq, dtype=F32)[:, None] * inv_freq[None, :]
    cos, sin = jnp.cos(ang), jnp.sin(ang)
    pad = group - rot
    c = jnp.concatenate([cos, cos, jnp.ones((seq, pad), F32)], axis=1)
    s1 = jnp.concatenate([-sin, jnp.zeros((seq, half + pad), F32)], axis=1)
    s2 = jnp.concatenate([jnp.zeros((seq, half), F32), sin, jnp.zeros((seq, pad), F32)], axis=1)
    rep = LANES // group
    return jnp.tile(c, (1, rep)), jnp.tile(s1, (1, rep)), jnp.tile(s2, (1, rep)), half


def _rope128(x, c, s1, s2, half):
    back = pltpu.roll(x, LANES - half, 1)
    fwd = pltpu.roll(x, half, 1)
    return x * c + back * s1 + fwd * s2


def _inproj_kernel(x_ref, mod_ref, g_ref, w_ref, wavt_ref, cw_ref, cb_ref, lng_ref, lnb_ref, sms_ref, smb_ref,
                   ac_ref, as1_ref, as2_ref, ic_ref, is1_ref, is2_ref,
                   qm_ref, km_ref, vm_ref, om_ref, aq_ref, ak_ref, avt_ref, iq_ref, ik_ref, sm_ref,
                   ubuf, *, a_half, i_half):
    ts = x_ref.shape[1]
    x = x_ref[0]
    shift = mod_ref[0, 0:1, :]
    scale = mod_ref[0, 1:2, :]
    y = x * lax.rsqrt(jnp.mean(x * x, axis=-1, keepdims=True) + NORM_EPS)
    h = (y * g_ref[...]) * (1.0 + scale) + shift
    hb = h.astype(BF16)

    def proj(lo, width):
        return _dot(hb, w_ref[:, lo:lo + width])

    @pl.when(pl.program_id(1) == 0)
    def _():
        ubuf[0:SUBLANES, :] = jnp.zeros((SUBLANES, 2 * M_WIDTH), F32)

    ubuf[SUBLANES:SUBLANES + ts, :] = proj(C_MQ, 2 * M_WIDTH)
    acc = cb_ref[...] + cw_ref[0:1, :] * ubuf[SUBLANES - 3:SUBLANES - 3 + ts, :]
    for j in range(1, CONV_WIDTH):
        off = SUBLANES - (CONV_WIDTH - 1) + j
        acc = acc + cw_ref[j:j + 1, :] * ubuf[off:off + ts, :]
    ubuf[0:SUBLANES, :] = ubuf[ts:ts + SUBLANES, :]
    qk = _silu(acc)
    qm_ref[0] = qk[:, :M_WIDTH].astype(BF16)
    km_ref[0] = (qk[:, M_WIDTH:] * (M_HEAD_DIM ** -0.5)).astype(BF16)

    vm_ref[0] = proj(C_MV, M_WIDTH).astype(BF16)
    om_ref[0] = proj(C_MO, M_WIDTH)

    ac, as1, as2 = ac_ref[...], as1_ref[...], as2_ref[...]
    aq = proj(C_AQ, A_WIDTH)
    for hd in range(A_HEADS):
        sl = slice(hd * A_HEAD_DIM, (hd + 1) * A_HEAD_DIM)
        aq_ref[0, :, sl] = _rope128(aq[:, sl], ac, as1, as2, a_half).astype(BF16)
    ak_ref[0] = _rope128(proj(C_AK, A_HEAD_DIM), ac, as1, as2, a_half).astype(BF16)
    avt_ref[0, 0] = _dot_nt(wavt_ref[...], hb).astype(BF16)

    ic, is1, is2 = ic_ref[...], is1_ref[...], is2_ref[...]
    iq = proj(C_IQ, IDX_HEADS * IDX_DIM)
    for blk in range(IDX_HEADS * IDX_DIM // LANES):
        sl = slice(blk * LANES, (blk + 1) * LANES)
        iq_ref[0, :, sl] = _rope128(iq[:, sl], ic, is1, is2, i_half).astype(BF16)

    sm = proj(C_SM, LANES)
    sm_ref[0] = sm * sms_ref[...] + smb_ref[...]
    lane = lax.broadcasted_iota(I32, (1, LANES), 1)
    is_k = lane < IDX_DIM
    mu = jnp.sum(jnp.where(is_k, sm, 0.0), axis=-1, keepdims=True) * (1.0 / IDX_DIM)
    dv = jnp.where(is_k, sm - mu, 0.0)
    var = jnp.sum(dv * dv, axis=-1, keepdims=True) * (1.0 / IDX_DIM)
    ikn = dv * lax.rsqrt(var + NORM_EPS) * lng_ref[...] + lnb_ref[...]
    ik_ref[0] = _rope128(ikn, ic, is1, is2, i_half)[:, :IDX_DIM].astype(BF16)


def _inproj(x, mod, g_pre, w_in, conv_w, conv_b, b_igate, b_fgate, g_idx_k, b_idx_k):
    bsz, seq, d = x.shape
    ts = min(ROW_TILE, seq)
    o_mi = 4 * M_WIDTH
    o_aq = o_mi + 2 * M_HEADS
    o_iw = o_aq + A_WIDTH + 2 * A_HEAD_DIM + IDX_HEADS * IDX_DIM + IDX_DIM
    w = jnp.concatenate([w_in[:, :o_mi], w_in[:, o_aq:o_iw], w_in[:, o_mi:o_aq], w_in[:, o_iw:],
                         jnp.zeros((d, IN_PAD - w_in.shape[1]), w_in.dtype)], axis=1).astype(BF16)
    idx_w_scale = (IDX_HEADS ** -0.5) * (IDX_DIM ** -0.5)
    sm_scale = jnp.ones((LANES,), F32).at[SM_IW:SM_IW + IDX_HEADS].set(idx_w_scale)
    sm_bias = jnp.zeros((LANES,), F32).at[SM_MI:SM_MI + M_HEADS].set(b_igate).at[SM_MF:SM_MF + M_HEADS].set(b_fgate)
    ln_g = jnp.zeros((LANES,), F32).at[:IDX_DIM].set(g_idx_k)
    ln_b = jnp.zeros((LANES,), F32).at[:IDX_DIM].set(b_idx_k)
    ac, as1, as2, a_half = _rope_tables(seq, A_HEAD_DIM, A_HEAD_DIM)
    ic, is1, is2, i_half = _rope_tables(seq, IDX_DIM, IDX_DIM)

    row = lambda width: pl.BlockSpec((1, ts, width), lambda b, s: (b, s, 0))
    vec = lambda width: pl.BlockSpec((1, width), lambda b, s: (0, 0))
    tab = pl.BlockSpec((ts, LANES), lambda b, s: (s, 0))
    outs = [(M_WIDTH, BF16), (M_WIDTH, BF16), (M_WIDTH, BF16), (M_WIDTH, F32), (A_WIDTH, BF16),
            (A_HEAD_DIM, BF16), None, (IDX_HEADS * IDX_DIM, BF16), (IDX_DIM, BF16), (LANES, F32)]
    avt_shape = jax.ShapeDtypeStruct((bsz, seq // ts, A_HEAD_DIM, ts), BF16)
    avt_spec = pl.BlockSpec((1, 1, A_HEAD_DIM, ts), lambda b, s: (b, s, 0, 0))
    return pl.pallas_call(
        functools.partial(_inproj_kernel, a_half=a_half, i_half=i_half),
        out_shape=[avt_shape if o is None else jax.ShapeDtypeStruct((bsz, seq, o[0]), o[1]) for o in outs],
        grid=(bsz, seq // ts),
        in_specs=[
            row(d),
            pl.BlockSpec((1, 6, d), lambda b, s: (b, 0, 0)),
            vec(d),
            pl.BlockSpec((d, IN_PAD), lambda b, s: (0, 0)),
            pl.BlockSpec((A_HEAD_DIM, d), lambda b, s: (0, 0)),
            pl.BlockSpec((CONV_WIDTH, 2 * M_WIDTH), lambda b, s: (0, 0)),
            vec(2 * M_WIDTH), vec(LANES), vec(LANES), vec(LANES), vec(LANES),
            tab, tab, tab, tab, tab, tab,
        ],
        out_specs=[avt_spec if o is None else row(o[0]) for o in outs],
        scratch_shapes=[pltpu.VMEM((ts + 2 * SUBLANES, 2 * M_WIDTH), F32)],
        compiler_params=_params(("arbitrary", "arbitrary")),
        name="inproj",
    )(x, mod, g_pre.reshape(1, d), w, w[:, C_AV:C_AV + A_HEAD_DIM].T, conv_w, conv_b.reshape(1, -1),
      ln_g.reshape(1, -1), ln_b.reshape(1, -1), sm_scale.reshape(1, -1), sm_bias.reshape(1, -1),
      ac, as1, as2, ic, is1, is2)


def _mlstm_kernel(q_ref, k_ref, v_ref, o_ref, gc_ref, gr_ref, gh_ref, h_ref, c_st, n_st, m_st):
    L = q_ref.shape[1]

    @pl.when(pl.program_id(1) == 0)
    def _():
        c_st[...] = jnp.zeros(c_st.shape, F32)
        n_st[...] = jnp.zeros(n_st.shape, F32)
        m_st[...] = jnp.full(m_st.shape, STAB_INIT, F32)

    row_i = lax.broadcasted_iota(I32, (L, L), 0)
    col_i = lax.broadcasted_iota(I32, (L, L), 1)
    tril = row_i >= col_i
    tri_l = jnp.where(tril, 1.0, 0.0).astype(BF16)
    tri_u = jnp.where(row_i <= col_i, 1.0, 0.0).astype(BF16)

    gc = gc_ref[0]
    gr = gr_ref[0, 0]
    lf_c = _log_sigmoid(gc)
    lf_r = _log_sigmoid(gr)
    c_hi, c_lo = _split_bf16(lf_c)
    c_lo2 = (lf_c - c_hi.astype(F32) - c_lo.astype(F32)).astype(BF16)
    b_c = _dot(tri_l, c_hi) + (_dot(tri_l, c_lo) + _dot(tri_l, c_lo2))
    r_hi, r_lo = _split_bf16(lf_r)
    r_lo2 = (lf_r - r_hi.astype(F32) - r_lo.astype(F32)).astype(BF16)
    b_r = _dot(r_hi, tri_u) + (_dot(r_lo, tri_u) + _dot(r_lo2, tri_u))

    for hd in range(M_HEADS):
        sl = slice(hd * M_HEAD_DIM, (hd + 1) * M_HEAD_DIM)
        q = q_ref[0, :, sl]
        k = k_ref[0, :, sl]
        v = v_ref[0, :, sl]
        i_col = gc[:, SM_MI + hd:SM_MI + hd + 1]
        b_col = b_c[:, SM_MF + hd:SM_MF + hd + 1]
        i_row = gr[hd:hd + 1, :]
        b_row = b_r[M_HEADS + hd:M_HEADS + hd + 1, :]
        m_prev = m_st[hd][:, 0:1]
        c_prev = c_st[hd]
        n_prev = n_st[hd]

        log_d = jnp.where(tril, (b_col - b_row) + i_row, -jnp.inf)
        m_inter = b_col + m_prev
        m_t = jnp.maximum(m_inter, jnp.max(log_d, axis=-1, keepdims=True))
        dmat = jnp.exp(log_d - m_t)
        s_qk = _dot_nt(q, k) * dmat
        inter = jnp.exp(m_inter - m_t)
        qf = q.astype(F32)
        num = _dot(s_qk.astype(BF16), v) + inter * _dot_nt(q, c_prev.astype(BF16))
        den = jnp.sum(s_qk, axis=-1, keepdims=True) + inter * jnp.sum(qf * n_prev, axis=-1, keepdims=True)
        hh = num / jnp.maximum(jnp.abs(den), jnp.exp(-m_t))

        b_last = b_col[L - 1:L, :]
        log_w = (b_last - b_col) + i_col
        m_new = jnp.maximum(b_last + m_prev, jnp.max(log_w, axis=0, keepdims=True))
        w = jnp.exp(log_w - m_new)
        decay = jnp.exp((b_last + m_prev) - m_new)
        vw = (v.astype(F32) * w).astype(BF16)
        c_st[hd] = decay * c_prev + lax.dot_general(vw, k, (((0,), (0,)), ((), ())), preferred_element_type=F32)
        n_st[hd] = decay * n_prev + jnp.sum(w * k.astype(F32), axis=0, keepdims=True)
        m_st[hd] = jnp.broadcast_to(m_new, (1, LANES))

        hn = hh * lax.rsqrt(jnp.mean(hh * hh, axis=-1, keepdims=True) + NORM_EPS) * gh_ref[:, sl]
        h_ref[0, :, sl] = (_sigmoid(o_ref[0, :, sl]) * hn).astype(BF16)


def _mlstm(qm, km, vm, om, small, g_head):
    bsz, seq, _ = qm.shape
    L = min(MLSTM_CHUNK, seq)
    nc = seq // L
    gr = small[:, :, SM_MI:SM_MI + 2 * M_HEADS].reshape(bsz, nc, L, 2 * M_HEADS).transpose(0, 1, 3, 2)
    row = lambda width: pl.BlockSpec((1, L, width), lambda b, c: (b, c, 0))
    return pl.pallas_call(
        _mlstm_kernel,
        out_shape=jax.ShapeDtypeStruct((bsz, seq, M_WIDTH), BF16),
        grid=(bsz, nc),
        in_specs=[row(M_WIDTH), row(M_WIDTH), row(M_WIDTH), row(M_WIDTH), row(LANES),
                  pl.BlockSpec((1, 1, 2 * M_HEADS, L), lambda b, c: (b, c, 0, 0)),
                  pl.BlockSpec((1, M_WIDTH), lambda b, c: (0, 0))],
        out_specs=row(M_WIDTH),
        scratch_shapes=[pltpu.VMEM((M_HEADS, M_HEAD_DIM, M_HEAD_DIM), F32),
                        pltpu.VMEM((M_HEADS, 1, M_HEAD_DIM), F32),
                        pltpu.VMEM((M_HEADS, 1, LANES), F32)],
        compiler_params=_params(("arbitrary", "arbitrary")),
        name="mlstm",
    )(qm, km, vm, om, small, gr, g_head.reshape(1, -1))


def _fold_rows(x, op):
    rows, lanes = x.shape
    x = x.reshape(rows // SUBLANES, SUBLANES, lanes)
    assert (rows // SUBLANES) & (rows // SUBLANES - 1) == 0, "row-vreg count must be a power of two"
    while x.shape[0] > 1:
        half = x.shape[0] // 2
        x = op(x[:half], x[half:])
    return x[0]


def _sortable(bits):
    return bits ^ ((bits >> 31) & jnp.int32(0x7FFFFFFF))


_KEY_NEG_INF = int(np.array([-np.inf], np.float32).view(np.int32)[0]) ^ 0x7FFFFFFF
_KEY_NEG_INF = _KEY_NEG_INF - (1 << 32) if _KEY_NEG_INF >= (1 << 31) else _KEY_NEG_INF
_INT_MIN = -(1 << 31)
_MIN_NORMAL_KEY = 1 << 23
_NEG_BIG = -1e30


def _dsa_kernel(iq_ref, iwt_ref, aq_ref, ik_ref, ak_ref, avt_ref, o_ref,
                keys_ref, tk_ref, bias_ref, lg_ref, p_ref, acc_ref, a_ref, m_ref, l_ref, *, k_top, kt, seq):
    qb = pl.program_id(1)
    nq = iq_ref.shape[1]
    nkt = (qb * nq) // kt + 1
    qlane = lax.broadcasted_iota(I32, (1, nq), 1)
    lim = qb * nq + (qlane // CHUNK + 1) * CHUNK
    krow = lax.broadcasted_iota(I32, (kt, nq), 0)
    iwt = iwt_ref[0]
    iqs = jnp.concatenate([iq_ref[0, :, hd * IDX_DIM:(hd + 1) * IDX_DIM] for hd in range(IDX_HEADS)], axis=0)

    def score_body(j, carry):
        start = pl.multiple_of(j * kt, kt)
        lg_ref[...] = _dot_nt(ik_ref[0, pl.ds(start, kt), :], iqs)
        sc = jnp.zeros((kt, nq), F32)
        for hd in range(IDX_HEADS):
            sc = sc + iwt[hd:hd + 1, :] * jnp.maximum(lg_ref[:, hd * nq:(hd + 1) * nq], 0.0)
        key = _sortable(pltpu.bitcast(sc, I32))
        key = jnp.where(sc == 0.0, seq - (start + krow), key)
        keys_ref[j] = jnp.where(start + krow < lim, key, _KEY_NEG_INF)
        return carry

    lax.fori_loop(0, nkt, score_body, 0)

    def count(ref, pred):
        def body(j, acc):
            return acc + _fold_rows(jnp.where(pred(ref[j]), 1.0, 0.0), jnp.add)
        part = lax.fori_loop(0, nkt, body, jnp.zeros((SUBLANES, nq), F32))
        return jnp.sum(part, axis=0, keepdims=True)

    kf = float(k_top)
    c0 = count(keys_ref, lambda key: key >= 0)
    t0 = jnp.where(c0 >= kf, 0, _INT_MIN).astype(I32)
    n0 = jnp.where(c0 >= kf, c0, (nkt * kt).astype(F32))

    def bit_body(i, carry):
        t, n_ge = carry
        cand = t | lax.shift_left(jnp.int32(1), 30 - i)
        c = count(keys_ref, lambda key: key >= cand)
        return jnp.where(c >= kf, cand, t), jnp.where(c >= kf, c, n_ge)

    t, n_ge = lax.fori_loop(0, 31, bit_body, (t0, n0))
    n_gt = count(keys_ref, lambda key: key > t)
    n_eq = n_ge - n_gt
    need = kf - n_gt
    tie = (n_eq > need) & (t > _KEY_NEG_INF)

    def tie_search():
        def fill(j, carry):
            tk_ref[j] = jnp.where(keys_ref[j] == t, (seq - 1) - (j * kt + krow), -1)
            return carry

        lax.fori_loop(0, nkt, fill, 0)
        u = jnp.zeros((1, nq), I32)
        for b in reversed(range(max(1, int(seq - 1).bit_length()))):
            cand = u | (1 << b)
            u = jnp.where(count(tk_ref, lambda v: v >= cand) >= need, cand, u)
        return (seq - 1) - u

    any_tie = jnp.max(jnp.where(tie, 1.0, 0.0)) > 0.0
    p_idx = lax.cond(any_tie, tie_search, lambda: jnp.full((1, nq), seq, I32))
    p_lim = jnp.where(tie, p_idx, jnp.where(t > _KEY_NEG_INF, seq, -1))

    acc_ref[...] = jnp.zeros(acc_ref.shape, F32)
    m_ref[...] = jnp.full(m_ref.shape, _NEG_BIG, F32)
    l_ref[...] = jnp.zeros(l_ref.shape, F32)
    aqs = jnp.concatenate([aq_ref[0, :, hd * A_HEAD_DIM:(hd + 1) * A_HEAD_DIM] for hd in range(A_HEADS)], axis=0)
    scale = A_HEAD_DIM ** -0.5

    def att_body(j, carry):
        start = pl.multiple_of(j * kt, kt)
        key = keys_ref[j]
        sel = (key > t) | ((key == t) & (start + krow <= p_lim))
        bias_ref[...] = jnp.where(sel, 0.0, 3.0 * _NEG_BIG)
        lg_ref[...] = _dot_nt(ak_ref[0, pl.ds(start, kt), :], aqs)
        for hd in range(A_HEADS):
            sl = slice(hd * nq, (hd + 1) * nq)
            lg = lg_ref[:, sl] * scale + bias_ref[...]
            m_old = m_ref[:, sl]
            m_new = jnp.maximum(m_old, jnp.max(_fold_rows(lg, jnp.maximum), axis=0, keepdims=True))
            p = jnp.exp(lg - m_new)
            p_ref[:, sl] = p.astype(BF16)
            alpha = jnp.exp(m_old - m_new)
            a_ref[:, sl] = alpha
            l_ref[:, sl] = alpha * l_ref[:, sl] + jnp.sum(_fold_rows(p, jnp.add), axis=0, keepdims=True)
            m_ref[:, sl] = m_new
        acc_ref[...] = a_ref[...] * acc_ref[...] + _dot(avt_ref[0, j], p_ref[...])
        return carry

    lax.fori_loop(0, nkt, att_body, 0)
    out = acc_ref[...] / l_ref[...]
    for hd in range(A_HEADS):
        o_ref[0, :, hd * A_HEAD_DIM:(hd + 1) * A_HEAD_DIM] = out[:, hd * nq:(hd + 1) * nq].T.astype(BF16)


def _dsa(iq, ik, small, aq, ak, avt):
    bsz, seq, _ = aq.shape
    nq = min(Q_BLOCK, seq)
    kt = avt.shape[3]
    k_top = min(TOPK_KEYS_MAX, seq // 4)
    assert kt >= k_top and kt % nq == 0
    assert seq < _MIN_NORMAL_KEY, "position keys of zero scores must stay below every normal positive score's key"
    iwt = small[:, :, SM_IW:SM_IW + SUBLANES].transpose(0, 2, 1)
    qrow = lambda width: pl.BlockSpec((1, nq, width), lambda b, q: (b, q, 0))
    full = lambda width: pl.BlockSpec((1, seq, width), lambda b, q: (b, 0, 0))
    return pl.pallas_call(
        functools.partial(_dsa_kernel, k_top=k_top, kt=kt, seq=seq),
        out_shape=jax.ShapeDtypeStruct((bsz, seq, A_WIDTH), BF16),
        grid=(bsz, seq // nq),
        in_specs=[qrow(IDX_HEADS * IDX_DIM),
                  pl.BlockSpec((1, SUBLANES, nq), lambda b, q: (b, 0, q)),
                  qrow(A_WIDTH), full(IDX_DIM), full(A_HEAD_DIM),
                  pl.BlockSpec((1, seq // kt, A_HEAD_DIM, kt), lambda b, q: (b, 0, 0, 0))],
        out_specs=qrow(A_WIDTH),
        scratch_shapes=[pltpu.VMEM((seq // kt, kt, nq), I32),
                        pltpu.VMEM((seq // kt, kt, nq), I32),
                        pltpu.VMEM((kt, nq), F32),
                        pltpu.VMEM((kt, A_HEADS * nq), F32),
                        pltpu.VMEM((kt, A_HEADS * nq), BF16),
                        pltpu.VMEM((A_HEAD_DIM, A_HEADS * nq), F32),
                        pltpu.VMEM((1, A_HEADS * nq), F32),
                        pltpu.VMEM((1, A_HEADS * nq), F32),
                        pltpu.VMEM((1, A_HEADS * nq), F32)],
        compiler_params=_params(("arbitrary", "arbitrary")),
        name="dsa",
    )(iq, iwt, aq, ik, ak, avt)


def _pack_bf16_pair(lo, hi):
    lo_b = pltpu.bitcast(lo.astype(BF16).astype(F32), U32)
    hi_b = pltpu.bitcast(hi.astype(BF16).astype(F32), U32)
    return (lo_b >> 16) | (hi_b & jnp.uint32(0xFFFF0000))


def _unpack_bf16_pair(w):
    lo = pltpu.bitcast(w << 16, F32)
    hi = pltpu.bitcast(w & jnp.uint32(0xFFFF0000), F32)
    return lo, hi


def _outproj_kernel(hm_ref, ha_ref, x_ref, mod_ref, gpost_ref, gpre_ref, wo_ref, wr_ref,
                    x1_ref, h2_ref, lt_ref):
    gate = mod_ref[0, 2:3, :]
    shift2 = mod_ref[0, 3:4, :]
    scale2 = mod_ref[0, 4:5, :]
    y = _dot(hm_ref[0], wo_ref[0:M_WIDTH, :]) + _dot(ha_ref[0], wo_ref[M_WIDTH:, :])
    yn = y * lax.rsqrt(jnp.mean(y * y, axis=-1, keepdims=True) + NORM_EPS) * gpost_ref[...]
    x1 = x_ref[0] + gate * yn
    x1_ref[0] = x1
    xn = x1 * lax.rsqrt(jnp.mean(x1 * x1, axis=-1, keepdims=True) + NORM_EPS) * gpre_ref[...]
    h2 = xn * (1.0 + scale2) + shift2
    half = h2.shape[1] // 2
    h2_ref[0] = _pack_bf16_pair(h2[:, :half], h2[:, half:])
    hh, hl = _split_bf16(h2)
    wh, wl = _split_bf16(wr_ref[...])
    lt_ref[0] = _dot_nt(wh, hh) + (_dot_nt(wh, hl) + _dot_nt(wl, hh))


def _outproj(hm, ha, x, mod, g_post, g_ffn_pre, w_out, w_router):
    bsz, seq, d = x.shape
    ts = min(ROW_TILE, seq)
    row = lambda width: pl.BlockSpec((1, ts, width), lambda b, s: (b, s, 0))
    vec = pl.BlockSpec((1, d), lambda b, s: (0, 0))
    x1, h2p, lt = pl.pallas_call(
        _outproj_kernel,
        out_shape=[jax.ShapeDtypeStruct((bsz, seq, d), F32),
                   jax.ShapeDtypeStruct((bsz, seq, d // 2), U32),
                   jax.ShapeDtypeStruct((bsz, N_EXPERTS, seq), F32)],
        grid=(bsz, seq // ts),
        in_specs=[row(M_WIDTH), row(A_WIDTH), row(d),
                  pl.BlockSpec((1, 6, d), lambda b, s: (b, 0, 0)), vec, vec,
                  pl.BlockSpec((d, d), lambda b, s: (0, 0)),
                  pl.BlockSpec((N_EXPERTS, d), lambda b, s: (0, 0))],
        out_specs=[row(d), row(d // 2), pl.BlockSpec((1, N_EXPERTS, ts), lambda b, s: (b, 0, s))],
        compiler_params=_params(("arbitrary", "arbitrary")),
        name="outproj",
    )(hm, ha, x, mod, g_post.reshape(1, d), g_ffn_pre.reshape(1, d), w_out.astype(BF16), w_router.T)
    return x1, h2p, lt


def _route_kernel(lt_ref, br_ref, eid_ref, rnk_ref, gw_ref, cnt_ref, carry):
    first = (pl.program_id(0) == 0) & (pl.program_id(1) == 0)

    @pl.when(first)
    def _():
        carry[...] = jnp.zeros(carry.shape, F32)

    tsr = lt_ref.shape[2]
    scores = _sigmoid(lt_ref[0])
    sel = scores + br_ref[...]
    x3 = sel.reshape(N_GROUPS, GROUP_SIZE, tsr)
    io3 = lax.broadcasted_iota(I32, x3.shape, 1)
    m1 = jnp.max(x3, axis=1, keepdims=True)
    i1 = jnp.min(jnp.where(x3 == m1, io3, GROUP_SIZE), axis=1, keepdims=True)
    m2 = jnp.max(jnp.where(io3 == i1, -jnp.inf, x3), axis=1, keepdims=True)
    gs = (m1 + m2).reshape(N_GROUPS, tsr)
    iog = lax.broadcasted_iota(I32, gs.shape, 0)
    g_rank = jnp.zeros(gs.shape, F32)
    for gp in range(N_GROUPS):
        r = gs[gp:gp + 1, :]
        g_rank = g_rank + jnp.where((r > gs) | ((r == gs) & (iog > gp)), 1.0, 0.0)
    g_sel = g_rank < float(TOPK_GROUPS)
    e_mask = jnp.broadcast_to(g_sel.reshape(N_GROUPS, 1, tsr), x3.shape).reshape(N_EXPERTS, tsr)
    v = jnp.where(e_mask, sel, -jnp.inf)
    ioe = lax.broadcasted_iota(I32, v.shape, 0)
    e_rank = jnp.zeros(v.shape, F32)
    for ep in range(N_EXPERTS):
        r = v[ep:ep + 1, :]
        e_rank = e_rank + jnp.where((r > v) | ((r == v) & (ioe > ep)), 1.0, 0.0)
    e_sel = e_rank < float(TOPK_EXPERTS)
    picked = jnp.where(e_sel, scores, 0.0)
    wd = picked / jnp.sum(picked, axis=0, keepdims=True) * ROUTE_SCALE

    sel_b = jnp.where(e_sel, 1.0, 0.0).astype(BF16)
    r_i = lax.broadcasted_iota(I32, (tsr, tsr), 0)
    c_i = lax.broadcasted_iota(I32, (tsr, tsr), 1)
    incl = _dot(sel_b, jnp.where(r_i <= c_i, 1.0, 0.0).astype(BF16))
    rank = carry[...] + incl - sel_b.astype(F32)
    carry[...] = carry[...] + incl[:, tsr - 1:tsr]
    cnt_ref[...] = jnp.broadcast_to(carry[...], cnt_ref.shape)
    e_r = lax.broadcasted_iota(I32, (N_EXPERTS, N_EXPERTS), 0)
    e_c = lax.broadcasted_iota(I32, (N_EXPERTS, N_EXPERTS), 1)
    slot = _dot(jnp.where(e_c < e_r, 1.0, 0.0).astype(BF16), sel_b)
    ioe_f = ioe.astype(F32)
    for kk in range(TOPK_EXPERTS):
        mk = e_sel & (slot == float(kk))
        eid_ref[0, kk:kk + 1, :] = jnp.sum(jnp.where(mk, ioe_f, 0.0), axis=0, keepdims=True).astype(I32)
        rnk_ref[0, kk:kk + 1, :] = jnp.sum(jnp.where(mk, rank, 0.0), axis=0, keepdims=True).astype(I32)
        gw_ref[0, kk:kk + 1, :] = jnp.sum(jnp.where(mk, wd, 0.0), axis=0, keepdims=True)


def _route(lt, b_router):
    bsz, _, seq = lt.shape
    tsr = min(ROUTE_TILE, seq)
    slab = pl.BlockSpec((1, TOPK_EXPERTS, tsr), lambda b, s: (b, 0, s))
    return pl.pallas_call(
        _route_kernel,
        out_shape=[jax.ShapeDtypeStruct((bsz, TOPK_EXPERTS, seq), I32),
                   jax.ShapeDtypeStruct((bsz, TOPK_EXPERTS, seq), I32),
                   jax.ShapeDtypeStruct((bsz, TOPK_EXPERTS, seq), F32),
                   jax.ShapeDtypeStruct((N_EXPERTS, LANES), F32)],
        grid=(bsz, seq // tsr),
        in_specs=[pl.BlockSpec((1, N_EXPERTS, tsr), lambda b, s: (b, 0, s)),
                  pl.BlockSpec((N_EXPERTS, 1), lambda b, s: (0, 0))],
        out_specs=[slab, slab, slab, pl.BlockSpec((N_EXPERTS, LANES), lambda b, s: (0, 0))],
        scratch_shapes=[pltpu.VMEM((N_EXPERTS, 1), F32)],
        compiler_params=_params(("arbitrary", "arbitrary")),
        name="route",
    )(lt, b_router.reshape(N_EXPERTS, 1))


def _dispatch_kernel(pos_ref, h2_ref, xs_in_ref, xs_ref, sem):
    del xs_in_ref
    tsd = h2_ref.shape[1]

    def row_copy(t, kk):
        return pltpu.make_async_copy(h2_ref.at[0, pl.ds(t, 1)], xs_ref.at[pl.ds(pos_ref[0, kk, t], 1)], sem)

    def body(t, carry):
        for kk in range(TOPK_EXPERTS):
            row_copy(t, kk).start()
        return carry

    lax.fori_loop(0, tsd, body, 0, unroll=DMA_ISSUE_UNROLL)
    for kk in range(TOPK_EXPERTS):
        pltpu.make_async_copy(h2_ref.at[0], xs_ref.at[pl.ds(0, tsd)], sem).wait()


def _dispatch(pos, h2p, rows):
    bsz, seq, dw = h2p.shape
    tsd = min(DISPATCH_TILE, seq)
    xs0 = jnp.zeros((rows, dw), U32)
    return pl.pallas_call(
        _dispatch_kernel,
        out_shape=jax.ShapeDtypeStruct((rows, dw), U32),
        grid=(bsz, seq // tsd),
        in_specs=[pl.BlockSpec((1, TOPK_EXPERTS, tsd), lambda b, s: (b, 0, s), memory_space=pltpu.SMEM),
                  pl.BlockSpec((1, tsd, dw), lambda b, s: (b, s, 0)),
                  pl.BlockSpec(memory_space=pl.ANY)],
        out_specs=pl.BlockSpec(memory_space=pl.ANY),
        scratch_shapes=[pltpu.SemaphoreType.DMA],
        input_output_aliases={2: 0},
        compiler_params=_params(("arbitrary", "arbitrary"), has_side_effects=True),
        name="dispatch",
    )(pos, h2p, xs0)


def _swiglu_packed(xp, wg_ref, wu_ref, wd_ref):
    lo, hi = _unpack_bf16_pair(xp)
    half = lo.shape[1]
    lo, hi = lo.astype(BF16), hi.astype(BF16)
    g = _dot(lo, wg_ref[0:half, :]) + _dot(hi, wg_ref[half:, :])
    u = _dot(lo, wu_ref[0:half, :]) + _dot(hi, wu_ref[half:, :])
    a = (_silu(g) * u).astype(BF16)
    return _dot(a, wd_ref[...])


def _expert_kernel(be_ref, xs_ref, wg_ref, wu_ref, wd_ref, ys_ref):
    del be_ref
    y = _swiglu_packed(xs_ref[...], wg_ref.at[0], wu_ref.at[0], wd_ref.at[0])
    half = y.shape[1] // 2
    ys_ref[...] = _pack_bf16_pair(y[:, :half], y[:, half:])


def _experts(block_e, xs, w_gate, w_up, w_down):
    rows, dw = xs.shape
    _, d, f = w_gate.shape
    nb = rows // EXPERT_ROWS
    return pl.pallas_call(
        _expert_kernel,
        out_shape=jax.ShapeDtypeStruct((rows, dw), U32),
        grid_spec=pltpu.PrefetchScalarGridSpec(
            num_scalar_prefetch=1,
            grid=(nb,),
            in_specs=[pl.BlockSpec((EXPERT_ROWS, dw), lambda i, be: (i, 0)),
                      pl.BlockSpec((1, d, f), lambda i, be: (be[i], 0, 0)),
                      pl.BlockSpec((1, d, f), lambda i, be: (be[i], 0, 0)),
                      pl.BlockSpec((1, f, d), lambda i, be: (be[i], 0, 0))],
            out_specs=pl.BlockSpec((EXPERT_ROWS, dw), lambda i, be: (i, 0)),
        ),
        compiler_params=_params(("arbitrary",)),
        name="experts",
    )(block_e, xs, w_gate.astype(BF16), w_up.astype(BF16), w_down.astype(BF16))


def _combine_kernel(pos_ref, gw_ref, x1_ref, h2_ref, mod_ref, g_ref, wsg_ref, wsu_ref, wsd_ref, ys_ref,
                    o_ref, buf, sem):
    tsc = h2_ref.shape[1]

    def row_copy(t, kk):
        return pltpu.make_async_copy(ys_ref.at[pl.ds(pos_ref[0, kk, t], 1)], buf.at[kk, pl.ds(t, 1)], sem)

    def body(t, carry):
        for kk in range(TOPK_EXPERTS):
            row_copy(t, kk).start()
        return carry

    lax.fori_loop(0, tsc, body, 0, unroll=DMA_ISSUE_UNROLL)
    shared = _swiglu_packed(h2_ref[0], wsg_ref, wsu_ref, wsd_ref)
    for kk in range(TOPK_EXPERTS):
        pltpu.make_async_copy(ys_ref.at[pl.ds(0, tsc)], buf.at[kk], sem).wait()

    half = shared.shape[1] // 2
    gw = gw_ref[0]
    y_lo, y_hi = shared[:, :half], shared[:, half:]
    for kk in range(TOPK_EXPERTS):
        lo, hi = _unpack_bf16_pair(buf[kk])
        w = gw[:, kk:kk + 1]
        y_lo = y_lo + w * lo
        y_hi = y_hi + w * hi
    ms = (jnp.sum(y_lo * y_lo, axis=-1, keepdims=True) + jnp.sum(y_hi * y_hi, axis=-1, keepdims=True)) * (0.5 / half)
    inv = lax.rsqrt(ms + NORM_EPS)
    gate = mod_ref[0, 5:6, :]
    g = g_ref[...]
    o_ref[0, :, :half] = x1_ref[0, :, :half] + gate[:, :half] * (y_lo * inv * g[:, :half])
    o_ref[0, :, half:] = x1_ref[0, :, half:] + gate[:, half:] * (y_hi * inv * g[:, half:])


def _combine(pos, gw_t, x1, h2p, mod, g_post, ws_gate, ws_up, ws_down, ys):
    bsz, seq, d = x1.shape
    dw = h2p.shape[2]
    f = ws_gate.shape[1]
    tsc = min(COMBINE_TILE, seq)
    row = lambda width: pl.BlockSpec((1, tsc, width), lambda b, s: (b, s, 0))
    const = lambda shape: pl.BlockSpec(shape, lambda b, s: tuple(0 for _ in shape))
    return pl.pallas_call(
        _combine_kernel,
        out_shape=jax.ShapeDtypeStruct((bsz, seq, d), F32),
        grid=(bsz, seq // tsc),
        in_specs=[pl.BlockSpec((1, TOPK_EXPERTS, tsc), lambda b, s: (b, 0, s), memory_space=pltpu.SMEM),
                  row(TOPK_EXPERTS), row(d), row(dw),
                  pl.BlockSpec((1, 6, d), lambda b, s: (b, 0, 0)),
                  const((1, d)), const((d, f)), const((d, f)), const((f, d)),
                  pl.BlockSpec(memory_space=pl.ANY)],
        out_specs=row(d),
        scratch_shapes=[pltpu.VMEM((TOPK_EXPERTS, tsc, dw), U32), pltpu.SemaphoreType.DMA],
        compiler_params=_params(("arbitrary", "arbitrary")),
        name="combine",
    )(pos, gw_t, x1, h2p, mod, g_post.reshape(1, d), ws_gate.astype(BF16), ws_up.astype(BF16),
      ws_down.astype(BF16), ys)


def _layer(x, c, w_ada, b_ada, g_mix_pre, g_mix_post, g_ffn_pre, g_ffn_post, w_in, conv_w, conv_b, b_igate, b_fgate,
           g_mlstm_head, g_idx_k, b_idx_k, w_out, w_router, b_router, w_exp_gate, w_exp_up, w_exp_down,
           w_sh_gate, w_sh_up, w_sh_down):
    bsz, seq, d = x.shape
    mod = _adaln(c, w_ada, b_ada).reshape(bsz, 6, d)
    qm, km, vm, om, aq, ak, avt, iq, ik, small = _inproj(x, mod, g_mix_pre, w_in, conv_w, conv_b, b_igate, b_fgate,
                                                        g_idx_k, b_idx_k)
    hm = _mlstm(qm, km, vm, om, small, g_mlstm_head)
    ha = _dsa(iq, ik, small, aq, ak, avt)
    x1, h2p, lt = _outproj(hm, ha, x, mod, g_mix_post, g_ffn_pre, w_out, w_router)

    eid, rnk, gw, cnt = _route(lt, b_router)
    counts = cnt[:, 0].astype(I32)
    padded = (counts + EXPERT_ROWS - 1) // EXPERT_ROWS * EXPERT_ROWS
    pad_end = jnp.cumsum(padded)
    pad_start = pad_end - padded
    pos = rnk
    for e in range(N_EXPERTS):
        pos = pos + jnp.where(eid == e, pad_start[e], 0)
    nb = (bsz * seq * TOPK_EXPERTS + EXPERT_ROWS - 1) // EXPERT_ROWS + N_EXPERTS
    block_start = jnp.arange(nb, dtype=I32) * EXPERT_ROWS
    block_e = jnp.minimum(jnp.sum((pad_end[None, :] <= block_start[:, None]).astype(I32), axis=1), N_EXPERTS - 1)

    xs = _dispatch(pos, h2p, nb * EXPERT_ROWS)
    ys = _experts(block_e, xs, w_exp_gate, w_exp_up, w_exp_down)
    return _combine(pos, gw.transpose(0, 2, 1), x1, h2p, mod, g_ffn_post, w_sh_gate, w_sh_up, w_sh_down, ys)


def kernel(x, c, w_ada, b_ada, g_mix_pre, g_mix_post, g_ffn_pre, g_ffn_post, w_in, conv_w, conv_b, b_igate, b_fgate, g_mlstm_head, g_idx_k, b_idx_k, w_out, w_router, b_router, w_exp_gate, w_exp_up, w_exp_down, w_sh_gate, w_sh_up, w_sh_down):
    per_layer = (w_ada, b_ada, g_mix_pre, g_mix_post, g_ffn_pre, g_ffn_post, w_in, conv_w, conv_b, b_igate, b_fgate,
                 g_mlstm_head, g_idx_k, b_idx_k, w_out, w_router, b_router, w_exp_gate, w_exp_up, w_exp_down,
                 w_sh_gate, w_sh_up, w_sh_down)
    for layer in range(w_ada.shape[0]):
        x = _layer(x, c, *(p[layer] for p in per_layer))
    return x
```

```python
import functools

import jax
import jax.numpy as jnp
import numpy as np
from jax import lax
from jax.experimental import pallas as pl
from jax.experimental.pallas import tpu as pltpu

F32 = jnp.float32
BF16 = jnp.bfloat16
I32 = jnp.int32
U32 = jnp.uint32

CHUNK = 64
M_HEADS = 4
M_HEAD_DIM = 128
M_WIDTH = M_HEADS * M_HEAD_DIM
CONV_WIDTH = 4
A_HEADS = 4
A_HEAD_DIM = 128
A_WIDTH = A_HEADS * A_HEAD_DIM
IDX_HEADS = 4
IDX_DIM = 64
TOPK_KEYS_MAX = 256
Q_BLOCK = 256
ROPE_THETA = 500000.0
ROPE_FRACTION_DIV = 4
N_EXPERTS = 64
N_GROUPS = 8
GROUP_SIZE = N_EXPERTS // N_GROUPS
TOPK_GROUPS = 4
TOPK_EXPERTS = 8
D_EXPERT = 256
ROUTE_SCALE = 2.5
NORM_EPS = 1e-6
STAB_INIT = -1e30

LANES = 128
SUBLANES = 8
VMEM_LIMIT = 56 * 1024 * 1024

ROW_TILE = 512
MLSTM_CHUNK = 256
ROUTE_TILE = 512
EXPERT_ROWS = 1024
DISPATCH_TILE = 256
COMBINE_TILE = 256
DMA_ISSUE_UNROLL = 4

C_MQ = 0
C_MV = 2 * M_WIDTH
C_MO = 3 * M_WIDTH
C_AQ = 4 * M_WIDTH
C_AK = C_AQ + A_WIDTH
C_AV = C_AK + A_HEAD_DIM
C_IQ = C_AV + A_HEAD_DIM
C_SM = C_IQ + IDX_HEADS * IDX_DIM
IN_PAD = C_SM + LANES
SM_MI = IDX_DIM
SM_MF = IDX_DIM + M_HEADS
SM_IW = IDX_DIM + 2 * M_HEADS


def _params(sem, **kw):
    return pltpu.CompilerParams(dimension_semantics=sem, vmem_limit_bytes=VMEM_LIMIT, **kw)


def _split_bf16(x):
    hi = x.astype(BF16)
    lo = (x - hi.astype(F32)).astype(BF16)
    return hi, lo


def _dot(a, b):
    return jnp.dot(a, b, preferred_element_type=F32)


def _dot_nt(a, b):
    return lax.dot_general(a, b, (((1,), (1,)), ((), ())), preferred_element_type=F32)


def _dot3(a, b):
    ah, al = _split_bf16(a)
    bh, bl = _split_bf16(b)
    return _dot(ah, bh) + (_dot(ah, bl) + _dot(al, bh))


def _sigmoid(x):
    return 1.0 / (1.0 + jnp.exp(-x))


def _silu(x):
    return x * _sigmoid(x)


def _log_sigmoid(x):
    return jnp.minimum(x, 0.0) - jnp.log(1.0 + jnp.exp(-jnp.abs(x)))


def _adaln_kernel(c_ref, w_ref, b_ref, o_ref):
    o_ref[...] = _dot3(_silu(c_ref[...]), w_ref[...]) + b_ref[...]


def _adaln(c, w, b):
    bsz, d = c.shape
    n = w.shape[1]
    tn = d
    return pl.pallas_call(
        _adaln_kernel,
        out_shape=jax.ShapeDtypeStruct((bsz, n), F32),
        grid=(n // tn,),
        in_specs=[
            pl.BlockSpec((bsz, d), lambda j: (0, 0)),
            pl.BlockSpec((d, tn), lambda j: (0, j)),
            pl.BlockSpec((1, tn), lambda j: (0, j)),
        ],
        out_specs=pl.BlockSpec((bsz, tn), lambda j: (0, j)),
        compiler_params=_params(("arbitrary",)),
        name="adaln",
    )(c, w, b.reshape(1, n))


def _rope_tables(seq, head_dim, group):
    rot = head_dim // ROPE_FRACTION_DIV
    half = rot // 2
    inv_freq = jnp.exp(-jnp.log(jnp.float32(ROPE_THETA)) * jnp.arange(half, dtype=F32) * (2.0 / rot))
    ang = jnp.arange(seq, dtype=F32)[:, None] * inv_freq[None, :]
    cos, sin = jnp.cos(ang), jnp.sin(ang)
    pad = group - rot
    c = jnp.concatenate([cos, cos, jnp.ones((seq, pad), F32)], axis=1)
    s1 = jnp.concatenate([-sin, jnp.zeros((seq, half + pad), F32)], axis=1)
    s2 = jnp.concatenate([jnp.zeros((seq, half), F32), sin, jnp.zeros((seq, pad), F32)], axis=1)
    rep = LANES // group
    return jnp.tile(c, (1, rep)), jnp.tile(s1, (1, rep)), jnp.tile(s2, (1, rep)), half


def _rope128(x, c, s1, s2, half):
    back = pltpu.roll(x, LANES - half, 1)
    fwd = pltpu.roll(x, half, 1)
    return x * c + back * s1 + fwd * s2


def _inproj_kernel(x_ref, mod_ref, g_ref, w_ref, wavt_ref, cw_ref, cb_ref, lng_ref, lnb_ref, sms_ref, smb_ref,
                   ac_ref, as1_ref, as2_ref, ic_ref, is1_ref, is2_ref,
                   qm_ref, km_ref, vm_ref, om_ref, aq_ref, ak_ref, avt_ref, iq_ref, ik_ref, sm_ref,
                   ubuf, *, a_half, i_half):
    ts = x_ref.shape[1]
    x = x_ref[0]
    shift = mod_ref[0, 0:1, :]
    scale = mod_ref[0, 1:2, :]
    y = x * lax.rsqrt(jnp.mean(x * x, axis=-1, keepdims=True) + NORM_EPS)
    h = (y * g_ref[...]) * (1.0 + scale) + shift
    hb = h.astype(BF16)

    def proj(lo, width):
        return _dot(hb, w_ref[:, lo:lo + width])

    @pl.when(pl.program_id(1) == 0)
    def _():
        ubuf[0:SUBLANES, :] = jnp.zeros((SUBLANES, 2 * M_WIDTH), F32)

    ubuf[SUBLANES:SUBLANES + ts, :] = proj(C_MQ, 2 * M_WIDTH)
    acc = cb_ref[...] + cw_ref[0:1, :] * ubuf[SUBLANES - 3:SUBLANES - 3 + ts, :]
    for j in range(1, CONV_WIDTH):
        off = SUBLANES - (CONV_WIDTH - 1) + j
        acc = acc + cw_ref[j:j + 1, :] * ubuf[off:off + ts, :]
    ubuf[0:SUBLANES, :] = ubuf[ts:ts + SUBLANES, :]
    qk = _silu(acc)
    qm_ref[0] = qk[:, :M_WIDTH].astype(BF16)
    km_ref[0] = (qk[:, M_WIDTH:] * (M_HEAD_DIM ** -0.5)).astype(BF16)

    vm_ref[0] = proj(C_MV, M_WIDTH).astype(BF16)
    om_ref[0] = proj(C_MO, M_WIDTH)

    ac, as1, as2 = ac_ref[...], as1_ref[...], as2_ref[...]
    aq = proj(C_AQ, A_WIDTH)
    for hd in range(A_HEADS):
        sl = slice(hd * A_HEAD_DIM, (hd + 1) * A_HEAD_DIM)
        aq_ref[0, :, sl] = _rope128(aq[:, sl], ac, as1, as2, a_half).astype(BF16)
    ak_ref[0] = _rope128(proj(C_AK, A_HEAD_DIM), ac, as1, as2, a_half).astype(BF16)
    avt_ref[0, 0] = _dot_nt(wavt_ref[...], hb).astype(BF16)

    ic, is1, is2 = ic_ref[...], is1_ref[...], is2_ref[...]
    iq = proj(C_IQ, IDX_HEADS * IDX_DIM)
    for blk in range(IDX_HEADS * IDX_DIM // LANES):
        sl = slice(blk * LANES, (blk + 1) * LANES)
        iq_ref[0, :, sl] = _rope128(iq[:, sl], ic, is1, is2, i_half).astype(BF16)

    sm = proj(C_SM, LANES)
    sm_ref[0] = sm * sms_ref[...] + smb_ref[...]
    lane = lax.broadcasted_iota(I32, (1, LANES), 1)
    is_k = lane < IDX_DIM
    mu = jnp.sum(jnp.where(is_k, sm, 0.0), axis=-1, keepdims=True) * (1.0 / IDX_DIM)
    dv = jnp.where(is_k, sm - mu, 0.0)
    var = jnp.sum(dv * dv, axis=-1, keepdims=True) * (1.0 / IDX_DIM)
    ikn = dv * lax.rsqrt(var + NORM_EPS) * lng_ref[...] + lnb_ref[...]
    ik_ref[0] = _rope128(ikn, ic, is1, is2, i_half)[:, :IDX_DIM].astype(BF16)


def _inproj(x, mod, g_pre, w_in, conv_w, conv_b, b_igate, b_fgate, g_idx_k, b_idx_k):
    bsz, seq, d = x.shape
    ts = min(ROW_TILE, seq)
    o_mi = 4 * M_WIDTH
    o_aq = o_mi + 2 * M_HEADS
    o_iw = o_aq + A_WIDTH + 2 * A_HEAD_DIM + IDX_HEADS * IDX_DIM + IDX_DIM
    w = jnp.concatenate([w_in[:, :o_mi], w_in[:, o_aq:o_iw], w_in[:, o_mi:o_aq], w_in[:, o_iw:],
                         jnp.zeros((d, IN_PAD - w_in.shape[1]), w_in.dtype)], axis=1).astype(BF16)
    idx_w_scale = (IDX_HEADS ** -0.5) * (IDX_DIM ** -0.5)
    sm_scale = jnp.ones((LANES,), F32).at[SM_IW:SM_IW + IDX_HEADS].set(idx_w_scale)
    sm_bias = jnp.zeros((LANES,), F32).at[SM_MI:SM_MI + M_HEADS].set(b_igate).at[SM_MF:SM_MF + M_HEADS].set(b_fgate)
    ln_g = jnp.zeros((LANES,), F32).at[:IDX_DIM].set(g_idx_k)
    ln_b = jnp.zeros((LANES,), F32).at[:IDX_DIM].set(b_idx_k)
    ac, as1, as2, a_half = _rope_tables(seq, A_HEAD_DIM, A_HEAD_DIM)
    ic, is1, is2, i_half = _rope_tables(seq, IDX_DIM, IDX_DIM)

    row = lambda width: pl.BlockSpec((1, ts, width), lambda b, s: (b, s, 0))
    vec = lambda width: pl.BlockSpec((1, width), lambda b, s: (0, 0))
    tab = pl.BlockSpec((ts, LANES), lambda b, s: (s, 0))
    outs = [(M_WIDTH, BF16), (M_WIDTH, BF16), (M_WIDTH, BF16), (M_WIDTH, F32), (A_WIDTH, BF16),
            (A_HEAD_DIM, BF16), None, (IDX_HEADS * IDX_DIM, BF16), (IDX_DIM, BF16), (LANES, F32)]
    avt_shape = jax.ShapeDtypeStruct((bsz, seq // ts, A_HEAD_DIM, ts), BF16)
    avt_spec = pl.BlockSpec((1, 1, A_HEAD_DIM, ts), lambda b, s: (b, s, 0, 0))
    return pl.pallas_call(
        functools.partial(_inproj_kernel, a_half=a_half, i_half=i_half),
        out_shape=[avt_shape if o is None else jax.ShapeDtypeStruct((bsz, seq, o[0]), o[1]) for o in outs],
        grid=(bsz, seq // ts),
        in_specs=[
            row(d),
            pl.BlockSpec((1, 6, d), lambda b, s: (b, 0, 0)),
            vec(d),
            pl.BlockSpec((d, IN_PAD), lambda b, s: (0, 0)),
            pl.BlockSpec((A_HEAD_DIM, d), lambda b, s: (0, 0)),
            pl.BlockSpec((CONV_WIDTH, 2 * M_WIDTH), lambda b, s: (0, 0)),
            vec(2 * M_WIDTH), vec(LANES), vec(LANES), vec(LANES), vec(LANES),
            tab, tab, tab, tab, tab, tab,
        ],
        out_specs=[avt_spec if o is None else row(o[0]) for o in outs],
        scratch_shapes=[pltpu.VMEM((ts + 2 * SUBLANES, 2 * M_WIDTH), F32)],
        compiler_params=_params(("arbitrary", "arbitrary")),
        name="inproj",
    )(x, mod, g_pre.reshape(1, d), w, w[:, C_AV:C_AV + A_HEAD_DIM].T, conv_w, conv_b.reshape(1, -1),
      ln_g.reshape(1, -1), ln_b.reshape(1, -1), sm_scale.reshape(1, -1), sm_bias.reshape(1, -1),
      ac, as1, as2, ic, is1, is2)


def _mlstm_kernel(q_ref, k_ref, v_ref, o_ref, gc_ref, gr_ref, gh_ref, h_ref, c_st, n_st, m_st):
    L = q_ref.shape[1]

    @pl.when(pl.program_id(1) == 0)
    def _():
        c_st[...] = jnp.zeros(c_st.shape, F32)
        n_st[...] = jnp.zeros(n_st.shape, F32)
        m_st[...] = jnp.full(m_st.shape, STAB_INIT, F32)

    row_i = lax.broadcasted_iota(I32, (L, L), 0)
    col_i = lax.broadcasted_iota(I32, (L, L), 1)
    tril = row_i >= col_i
    tri_l = jnp.where(tril, 1.0, 0.0).astype(BF16)
    tri_u = jnp.where(row_i <= col_i, 1.0, 0.0).astype(BF16)

    gc = gc_ref[0]
    gr = gr_ref[0, 0]
    lf_c = _log_sigmoid(gc)
    lf_r = _log_sigmoid(gr)
    c_hi, c_lo = _split_bf16(lf_c)
    c_lo2 = (lf_c - c_hi.astype(F32) - c_lo.astype(F32)).astype(BF16)
    b_c = _dot(tri_l, c_hi) + (_dot(tri_l, c_lo) + _dot(tri_l, c_lo2))
    r_hi, r_lo = _split_bf16(lf_r)
    r_lo2 = (lf_r - r_hi.astype(F32) - r_lo.astype(F32)).astype(BF16)
    b_r = _dot(r_hi, tri_u) + (_dot(r_lo, tri_u) + _dot(r_lo2, tri_u))

    for hd in range(M_HEADS):
        sl = slice(hd * M_HEAD_DIM, (hd + 1) * M_HEAD_DIM)
        q = q_ref[0, :, sl]
        k = k_ref[0, :, sl]
        v = v_ref[0, :, sl]
        i_col = gc[:, SM_MI + hd:SM_MI + hd + 1]
        b_col = b_c[:, SM_MF + hd:SM_MF + hd + 1]
        i_row = gr[hd:hd + 1, :]
        b_row = b_r[M_HEADS + hd:M_HEADS + hd + 1, :]
        m_prev = m_st[hd][:, 0:1]
        c_prev = c_st[hd]
        n_prev = n_st[hd]

        log_d = jnp.where(tril, (b_col - b_row) + i_row, -jnp.inf)
        m_inter = b_col + m_prev
        m_t = jnp.maximum(m_inter, jnp.max(log_d, axis=-1, keepdims=True))
        dmat = jnp.exp(log_d - m_t)
        s_qk = _dot_nt(q, k) * dmat
        inter = jnp.exp(m_inter - m_t)
        qf = q.astype(F32)
        num = _dot(s_qk.astype(BF16), v) + inter * _dot_nt(q, c_prev.astype(BF16))
        den = jnp.sum(s_qk, axis=-1, keepdims=True) + inter * jnp.sum(qf * n_prev, axis=-1, keepdims=True)
        hh = num / jnp.maximum(jnp.abs(den), jnp.exp(-m_t))

        b_last = b_col[L - 1:L, :]
        log_w = (b_last - b_col) + i_col
        m_new = jnp.maximum(b_last + m_prev, jnp.max(log_w, axis=0, keepdims=True))
        w = jnp.exp(log_w - m_new)
        decay = jnp.exp((b_last + m_prev) - m_new)
        vw = (v.astype(F32) * w).astype(BF16)
        c_st[hd] = decay * c_prev + lax.dot_general(vw, k, (((0,), (0,)), ((), ())), preferred_element_type=F32)
        n_st[hd] = decay * n_prev + jnp.sum(w * k.astype(F32), axis=0, keepdims=True)
        m_st[hd] = jnp.broadcast_to(m_new, (1, LANES))

        hn = hh * lax.rsqrt(jnp.mean(hh * hh, axis=-1, keepdims=True) + NORM_EPS) * gh_ref[:, sl]
        h_ref[0, :, sl] = (_sigmoid(o_ref[0, :, sl]) * hn).astype(BF16)


def _mlstm(qm, km, vm, om, small, g_head):
    bsz, seq, _ = qm.shape
    L = min(MLSTM_CHUNK, seq)
    nc = seq // L
    gr = small[:, :, SM_MI:SM_MI + 2 * M_HEADS].reshape(bsz, nc, L, 2 * M_HEADS).transpose(0, 1, 3, 2)
    row = lambda width: pl.BlockSpec((1, L, width), lambda b, c: (b, c, 0))
    return pl.pallas_call(
        _mlstm_kernel,
        out_shape=jax.ShapeDtypeStruct((bsz, seq, M_WIDTH), BF16),
        grid=(bsz, nc),
        in_specs=[row(M_WIDTH), row(M_WIDTH), row(M_WIDTH), row(M_WIDTH), row(LANES),
                  pl.BlockSpec((1, 1, 2 * M_HEADS, L), lambda b, c: (b, c, 0, 0)),
                  pl.BlockSpec((1, M_WIDTH), lambda b, c: (0, 0))],
        out_specs=row(M_WIDTH),
        scratch_shapes=[pltpu.VMEM((M_HEADS, M_HEAD_DIM, M_HEAD_DIM), F32),
                        pltpu.VMEM((M_HEADS, 1, M_HEAD_DIM), F32),
                        pltpu.VMEM((M_HEADS, 1, LANES), F32)],
        compiler_params=_params(("arbitrary", "arbitrary")),
        name="mlstm",
    )(qm, km, vm, om, small, gr, g_head.reshape(1, -1))


def _fold_rows(x, op):
    rows, lanes = x.shape
    x = x.reshape(rows // SUBLANES, SUBLANES, lanes)
    assert (rows // SUBLANES) & (rows // SUBLANES - 1) == 0, "row-vreg count must be a power of two"
    while x.shape[0] > 1:
        half = x.shape[0] // 2
        x = op(x[:half], x[half:])
    return x[0]


def _sortable(bits):
    return bits ^ ((bits >> 31) & jnp.int32(0x7FFFFFFF))


_KEY_NEG_INF = int(np.array([-np.inf], np.float32).view(np.int32)[0]) ^ 0x7FFFFFFF
_KEY_NEG_INF = _KEY_NEG_INF - (1 << 32) if _KEY_NEG_INF >= (1 << 31) else _KEY_NEG_INF
_INT_MIN = -(1 << 31)
_MIN_NORMAL_KEY = 1 << 23
_NEG_BIG = -1e30


def _dsa_kernel(iq_ref, iwt_ref, aq_ref, ik_ref, ak_ref, avt_ref, o_ref,
                keys_ref, tk_ref, bias_ref, lg_ref, p_ref, acc_ref, a_ref, m_ref, l_ref, *, k_top, kt, seq):
    qb = pl.program_id(1)
    nq = iq_ref.shape[1]
    nkt = (qb * nq) // kt + 1
    qlane = lax.broadcasted_iota(I32, (1, nq), 1)
    lim = qb * nq + (qlane // CHUNK + 1) * CHUNK
    krow = lax.broadcasted_iota(I32, (kt, nq), 0)
    iwt = iwt_ref[0]
    iqs = jnp.concatenate([iq_ref[0, :, hd * IDX_DIM:(hd + 1) * IDX_DIM] for hd in range(IDX_HEADS)], axis=0)

    def score_body(j, carry):
        start = pl.multiple_of(j * kt, kt)
        lg_ref[...] = _dot_nt(ik_ref[0, pl.ds(start, kt), :], iqs)
        sc = jnp.zeros((kt, nq), F32)
        for hd in range(IDX_HEADS):
            sc = sc + iwt[hd:hd + 1, :] * jnp.maximum(lg_ref[:, hd * nq:(hd + 1) * nq], 0.0)
        key = _sortable(pltpu.bitcast(sc, I32))
        key = jnp.where(sc == 0.0, seq - (start + krow), key)
        keys_ref[j] = jnp.where(start + krow < lim, key, _KEY_NEG_INF)
        return carry

    lax.fori_loop(0, nkt, score_body, 0)

    def count(ref, pred):
        def body(j, acc):
            return acc + _fold_rows(jnp.where(pred(ref[j]), 1.0, 0.0), jnp.add)
        part = lax.fori_loop(0, nkt, body, jnp.zeros((SUBLANES, nq), F32))
        return jnp.sum(part, axis=0, keepdims=True)

    kf = float(k_top)
    c0 = count(keys_ref, lambda key: key >= 0)
    t0 = jnp.where(c0 >= kf, 0, _INT_MIN).astype(I32)
    n0 = jnp.where(c0 >= kf, c0, (nkt * kt).astype(F32))

    def bit_body(i, carry):
        t, n_ge = carry
        cand = t | lax.shift_left(jnp.int32(1), 30 - i)
        c = count(keys_ref, lambda key: key >= cand)
        return jnp.where(c >= kf, cand, t), jnp.where(c >= kf, c, n_ge)

    t, n_ge = lax.fori_loop(0, 31, bit_body, (t0, n0))
    n_gt = count(keys_ref, lambda key: key > t)
    n_eq = n_ge - n_gt
    need = kf - n_gt
    tie = (n_eq > need) & (t > _KEY_NEG_INF)

    def tie_search():
        def fill(j, carry):
            tk_ref[j] = jnp.where(keys_ref[j] == t, (seq - 1) - (j * kt + krow), -1)
            return carry

        lax.fori_loop(0, nkt, fill, 0)
        u = jnp.zeros((1, nq), I32)
        for b in reversed(range(max(1, int(seq - 1).bit_length()))):
            cand = u | (1 << b)
            u = jnp.where(count(tk_ref, lambda v: v >= cand) >= need, cand, u)
        return (seq - 1) - u

    any_tie = jnp.max(jnp.where(tie, 1.0, 0.0)) > 0.0
    p_idx = lax.cond(any_tie, tie_search, lambda: jnp.full((1, nq), seq, I32))
    p_lim = jnp.where(tie, p_idx, jnp.where(t > _KEY_NEG_INF, seq, -1))

    acc_ref[...] = jnp.zeros(acc_ref.shape, F32)
    m_ref[...] = jnp.full(m_ref.shape, _NEG_BIG, F32)
    l_ref[...] = jnp.zeros(l_ref.shape, F32)
    aqs = jnp.concatenate([aq_ref[0, :, hd * A_HEAD_DIM:(hd + 1) * A_HEAD_DIM] for hd in range(A_HEADS)], axis=0)
    scale = A_HEAD_DIM ** -0.5

    def att_body(j, carry):
        start = pl.multiple_of(j * kt, kt)
        key = keys_ref[j]
        sel = (key > t) | ((key == t) & (start + krow <= p_lim))
        bias_ref[...] = jnp.where(sel, 0.0, 3.0 * _NEG_BIG)
        lg_ref[...] = _dot_nt(ak_ref[0, pl.ds(start, kt), :], aqs)
        for hd in range(A_HEADS):
            sl = slice(hd * nq, (hd + 1) * nq)
            lg = lg_ref[:, sl] * scale + bias_ref[...]
            m_old = m_ref[:, sl]
            m_new = jnp.maximum(m_old, jnp.max(_fold_rows(lg, jnp.maximum), axis=0, keepdims=True))
            p = jnp.exp(lg - m_new)
            p_ref[:, sl] = p.astype(BF16)
            alpha = jnp.exp(m_old - m_new)
            a_ref[:, sl] = alpha
            l_ref[:, sl] = alpha * l_ref[:, sl] + jnp.sum(_fold_rows(p, jnp.add), axis=0, keepdims=True)
            m_ref[:, sl] = m_new
        acc_ref[...] = a_ref[...] * acc_ref[...] + _dot(avt_ref[0, j], p_ref[...])
        return carry

    lax.fori_loop(0, nkt, att_body, 0)
    out = acc_ref[...] / l_ref[...]
    for hd in range(A_HEADS):
        o_ref[0, :, hd * A_HEAD_DIM:(hd + 1) * A_HEAD_DIM] = out[:, hd * nq:(hd + 1) * nq].T.astype(BF16)


def _dsa(iq, ik, small, aq, ak, avt):
    bsz, seq, _ = aq.shape
    nq = min(Q_BLOCK, seq)
    kt = avt.shape[3]
    k_top = min(TOPK_KEYS_MAX, seq // 4)
    assert kt >= k_top and kt % nq == 0
    assert seq < _MIN_NORMAL_KEY, "position keys of zero scores must stay below every normal positive score's key"
    iwt = small[:, :, SM_IW:SM_IW + SUBLANES].transpose(0, 2, 1)
    qrow = lambda width: pl.BlockSpec((1, nq, width), lambda b, q: (b, q, 0))
    full = lambda width: pl.BlockSpec((1, seq, width), lambda b, q: (b, 0, 0))
    return pl.pallas_call(
        functools.partial(_dsa_kernel, k_top=k_top, kt=kt, seq=seq),
        out_shape=jax.ShapeDtypeStruct((bsz, seq, A_WIDTH), BF16),
        grid=(bsz, seq // nq),
        in_specs=[qrow(IDX_HEADS * IDX_DIM),
                  pl.BlockSpec((1, SUBLANES, nq), lambda b, q: (b, 0, q)),
                  qrow(A_WIDTH), full(IDX_DIM), full(A_HEAD_DIM),
                  pl.BlockSpec((1, seq // kt, A_HEAD_DIM, kt), lambda b, q: (b, 0, 0, 0))],
        out_specs=qrow(A_WIDTH),
        scratch_shapes=[pltpu.VMEM((seq // kt, kt, nq), I32),
                        pltpu.VMEM((seq // kt, kt, nq), I32),
                        pltpu.VMEM((kt, nq), F32),
                        pltpu.VMEM((kt, A_HEADS * nq), F32),
                        pltpu.VMEM((kt, A_HEADS * nq), BF16),
                        pltpu.VMEM((A_HEAD_DIM, A_HEADS * nq), F32),
                        pltpu.VMEM((1, A_HEADS * nq), F32),
                        pltpu.VMEM((1, A_HEADS * nq), F32),
                        pltpu.VMEM((1, A_HEADS * nq), F32)],
        compiler_params=_params(("arbitrary", "arbitrary")),
        name="dsa",
    )(iq, iwt, aq, ik, ak, avt)


def _pack_bf16_pair(lo, hi):
    lo_b = pltpu.bitcast(lo.astype(BF16).astype(F32), U32)
    hi_b = pltpu.bitcast(hi.astype(BF16).astype(F32), U32)
    return (lo_b >> 16) | (hi_b & jnp.uint32(0xFFFF0000))


def _unpack_bf16_pair(w):
    lo = pltpu.bitcast(w << 16, F32)
    hi = pltpu.bitcast(w & jnp.uint32(0xFFFF0000), F32)
    return lo, hi


def _outproj_kernel(hm_ref, ha_ref, x_ref, mod_ref, gpost_ref, gpre_ref, wo_ref, wr_ref,
                    x1_ref, h2_ref, lt_ref):
    gate = mod_ref[0, 2:3, :]
    shift2 = mod_ref[0, 3:4, :]
    scale2 = mod_ref[0, 4:5, :]
    y = _dot(hm_ref[0], wo_ref[0:M_WIDTH, :]) + _dot(ha_ref[0], wo_ref[M_WIDTH:, :])
    yn = y * lax.rsqrt(jnp.mean(y * y, axis=-1, keepdims=True) + NORM_EPS) * gpost_ref[...]
    x1 = x_ref[0] + gate * yn
    x1_ref[0] = x1
    xn = x1 * lax.rsqrt(jnp.mean(x1 * x1, axis=-1, keepdims=True) + NORM_EPS) * gpre_ref[...]
    h2 = xn * (1.0 + scale2) + shift2
    half = h2.shape[1] // 2
    h2_ref[0] = _pack_bf16_pair(h2[:, :half], h2[:, half:])
    hh, hl = _split_bf16(h2)
    wh, wl = _split_bf16(wr_ref[...])
    lt_ref[0] = _dot_nt(wh, hh) + (_dot_nt(wh, hl) + _dot_nt(wl, hh))


def _outproj(hm, ha, x, mod, g_post, g_ffn_pre, w_out, w_router):
    bsz, seq, d = x.shape
    ts = min(ROW_TILE, seq)
    row = lambda width: pl.BlockSpec((1, ts, width), lambda b, s: (b, s, 0))
    vec = pl.BlockSpec((1, d), lambda b, s: (0, 0))
    x1, h2p, lt = pl.pallas_call(
        _outproj_kernel,
        out_shape=[jax.ShapeDtypeStruct((bsz, seq, d), F32),
                   jax.ShapeDtypeStruct((bsz, seq, d // 2), U32),
                   jax.ShapeDtypeStruct((bsz, N_EXPERTS, seq), F32)],
        grid=(bsz, seq // ts),
        in_specs=[row(M_WIDTH), row(A_WIDTH), row(d),
                  pl.BlockSpec((1, 6, d), lambda b, s: (b, 0, 0)), vec, vec,
                  pl.BlockSpec((d, d), lambda b, s: (0, 0)),
                  pl.BlockSpec((N_EXPERTS, d), lambda b, s: (0, 0))],
        out_specs=[row(d), row(d // 2), pl.BlockSpec((1, N_EXPERTS, ts), lambda b, s: (b, 0, s))],
        compiler_params=_params(("arbitrary", "arbitrary")),
        name="outproj",
    )(hm, ha, x, mod, g_post.reshape(1, d), g_ffn_pre.reshape(1, d), w_out.astype(BF16), w_router.T)
    return x1, h2p, lt


def _route_kernel(lt_ref, br_ref, eid_ref, rnk_ref, gw_ref, cnt_ref, carry):
    first = (pl.program_id(0) == 0) & (pl.program_id(1) == 0)

    @pl.when(first)
    def _():
        carry[...] = jnp.zeros(carry.shape, F32)

    tsr = lt_ref.shape[2]
    scores = _sigmoid(lt_ref[0])
    sel = scores + br_ref[...]
    x3 = sel.reshape(N_GROUPS, GROUP_SIZE, tsr)
    io3 = lax.broadcasted_iota(I32, x3.shape, 1)
    m1 = jnp.max(x3, axis=1, keepdims=True)
    i1 = jnp.min(jnp.where(x3 == m1, io3, GROUP_SIZE), axis=1, keepdims=True)
    m2 = jnp.max(jnp.where(io3 == i1, -jnp.inf, x3), axis=1, keepdims=True)
    gs = (m1 + m2).reshape(N_GROUPS, tsr)
    iog = lax.broadcasted_iota(I32, gs.shape, 0)
    g_rank = jnp.zeros(gs.shape, F32)
    for gp in range(N_GROUPS):
        r = gs[gp:gp + 1, :]
        g_rank = g_rank + jnp.where((r > gs) | ((r == gs) & (iog > gp)), 1.0, 0.0)
    g_sel = g_rank < float(TOPK_GROUPS)
    e_mask = jnp.broadcast_to(g_sel.reshape(N_GROUPS, 1, tsr), x3.shape).reshape(N_EXPERTS, tsr)
    v = jnp.where(e_mask, sel, -jnp.inf)
    ioe = lax.broadcasted_iota(I32, v.shape, 0)
    e_rank = jnp.zeros(v.shape, F32)
    for ep in range(N_EXPERTS):
        r = v[ep:ep + 1, :]
        e_rank = e_rank + jnp.where((r > v) | ((r == v) & (ioe > ep)), 1.0, 0.0)
    e_sel = e_rank < float(TOPK_EXPERTS)
    picked = jnp.where(e_sel, scores, 0.0)
    wd = picked / jnp.sum(picked, axis=0, keepdims=True) * ROUTE_SCALE

    sel_b = jnp.where(e_sel, 1.0, 0.0).astype(BF16)
    r_i = lax.broadcasted_iota(I32, (tsr, tsr), 0)
    c_i = lax.broadcasted_iota(I32, (tsr, tsr), 1)
    incl = _dot(sel_b, jnp.where(r_i <= c_i, 1.0, 0.0).astype(BF16))
    rank = carry[...] + incl - sel_b.astype(F32)
    carry[...] = carry[...] + incl[:, tsr - 1:tsr]
    cnt_ref[...] = jnp.broadcast_to(carry[...], cnt_ref.shape)
    e_r = lax.broadcasted_iota(I32, (N_EXPERTS, N_EXPERTS), 0)
    e_c = lax.broadcasted_iota(I32, (N_EXPERTS, N_EXPERTS), 1)
    slot = _dot(jnp.where(e_c < e_r, 1.0, 0.0).astype(BF16), sel_b)
    ioe_f = ioe.astype(F32)
    for kk in range(TOPK_EXPERTS):
        mk = e_sel & (slot == float(kk))
        eid_ref[0, kk:kk + 1, :] = jnp.sum(jnp.where(mk, ioe_f, 0.0), axis=0, keepdims=True).astype(I32)
        rnk_ref[0, kk:kk + 1, :] = jnp.sum(jnp.where(mk, rank, 0.0), axis=0, keepdims=True).astype(I32)
        gw_ref[0, kk:kk + 1, :] = jnp.sum(jnp.where(mk, wd, 0.0), axis=0, keepdims=True)


def _route(lt, b_router):
    bsz, _, seq = lt.shape
    tsr = min(ROUTE_TILE, seq)
    slab = pl.BlockSpec((1, TOPK_EXPERTS, tsr), lambda b, s: (b, 0, s))
    return pl.pallas_call(
        _route_kernel,
        out_shape=[jax.ShapeDtypeStruct((bsz, TOPK_EXPERTS, seq), I32),
                   jax.ShapeDtypeStruct((bsz, TOPK_EXPERTS, seq), I32),
                   jax.ShapeDtypeStruct((bsz, TOPK_EXPERTS, seq), F32),
                   jax.ShapeDtypeStruct((N_EXPERTS, LANES), F32)],
        grid=(bsz, seq // tsr),
        in_specs=[pl.BlockSpec((1, N_EXPERTS, tsr), lambda b, s: (b, 0, s)),
                  pl.BlockSpec((N_EXPERTS, 1), lambda b, s: (0, 0))],
        out_specs=[slab, slab, slab, pl.BlockSpec((N_EXPERTS, LANES), lambda b, s: (0, 0))],
        scratch_shapes=[pltpu.VMEM((N_EXPERTS, 1), F32)],
        compiler_params=_params(("arbitrary", "arbitrary")),
        name="route",
    )(lt, b_router.reshape(N_EXPERTS, 1))


def _dispatch_kernel(pos_ref, h2_ref, xs_in_ref, xs_ref, sem):
    del xs_in_ref
    tsd = h2_ref.shape[1]

    def row_copy(t, kk):
        return pltpu.make_async_copy(h2_ref.at[0, pl.ds(t, 1)], xs_ref.at[pl.ds(pos_ref[0, kk, t], 1)], sem)

    def body(t, carry):
        for kk in range(TOPK_EXPERTS):
            row_copy(t, kk).start()
        return carry

    lax.fori_loop(0, tsd, body, 0, unroll=DMA_ISSUE_UNROLL)
    for kk in range(TOPK_EXPERTS):
        pltpu.make_async_copy(h2_ref.at[0], xs_ref.at[pl.ds(0, tsd)], sem).wait()


def _dispatch(pos, h2p, rows):
    bsz, seq, dw = h2p.shape
    tsd = min(DISPATCH_TILE, seq)
    xs0 = jnp.zeros((rows, dw), U32)
    return pl.pallas_call(
        _dispatch_kernel,
        out_shape=jax.ShapeDtypeStruct((rows, dw), U32),
        grid=(bsz, seq // tsd),
        in_specs=[pl.BlockSpec((1, TOPK_EXPERTS, tsd), lambda b, s: (b, 0, s), memory_space=pltpu.SMEM),
                  pl.BlockSpec((1, tsd, dw), lambda b, s: (b, s, 0)),
                  pl.BlockSpec(memory_space=pl.ANY)],
        out_specs=pl.BlockSpec(memory_space=pl.ANY),
        scratch_shapes=[pltpu.SemaphoreType.DMA],
        input_output_aliases={2: 0},
        compiler_params=_params(("arbitrary", "arbitrary"), has_side_effects=True),
        name="dispatch",
    )(pos, h2p, xs0)


def _swiglu_packed(xp, wg_ref, wu_ref, wd_ref):
    lo, hi = _unpack_bf16_pair(xp)
    half = lo.shape[1]
    lo, hi = lo.astype(BF16), hi.astype(BF16)
    g = _dot(lo, wg_ref[0:half, :]) + _dot(hi, wg_ref[half:, :])
    u = _dot(lo, wu_ref[0:half, :]) + _dot(hi, wu_ref[half:, :])
    a = (_silu(g) * u).astype(BF16)
    return _dot(a, wd_ref[...])


def _expert_kernel(be_ref, xs_ref, wg_ref, wu_ref, wd_ref, ys_ref):
    del be_ref
    y = _swiglu_packed(xs_ref[...], wg_ref.at[0], wu_ref.at[0], wd_ref.at[0])
    half = y.shape[1] // 2
    ys_ref[...] = _pack_bf16_pair(y[:, :half], y[:, half:])


def _experts(block_e, xs, w_gate, w_up, w_down):
    rows, dw = xs.shape
    _, d, f = w_gate.shape
    nb = rows // EXPERT_ROWS
    return pl.pallas_call(
        _expert_kernel,
        out_shape=jax.ShapeDtypeStruct((rows, dw), U32),
        grid_spec=pltpu.PrefetchScalarGridSpec(
            num_scalar_prefetch=1,
            grid=(nb,),
            in_specs=[pl.BlockSpec((EXPERT_ROWS, dw), lambda i, be: (i, 0)),
                      pl.BlockSpec((1, d, f), lambda i, be: (be[i], 0, 0)),
                      pl.BlockSpec((1, d, f), lambda i, be: (be[i], 0, 0)),
                      pl.BlockSpec((1, f, d), lambda i, be: (be[i], 0, 0))],
            out_specs=pl.BlockSpec((EXPERT_ROWS, dw), lambda i, be: (i, 0)),
        ),
        compiler_params=_params(("arbitrary",)),
        name="experts",
    )(block_e, xs, w_gate.astype(BF16), w_up.astype(BF16), w_down.astype(BF16))


def _combine_kernel(pos_ref, gw_ref, x1_ref, h2_ref, mod_ref, g_ref, wsg_ref, wsu_ref, wsd_ref, ys_ref,
                    o_ref, buf, sem):
    tsc = h2_ref.shape[1]

    def row_copy(t, kk):
        return pltpu.make_async_copy(ys_ref.at[pl.ds(pos_ref[0, kk, t], 1)], buf.at[kk, pl.ds(t, 1)], sem)

    def body(t, carry):
        for kk in range(TOPK_EXPERTS):
            row_copy(t, kk).start()
        return carry

    lax.fori_loop(0, tsc, body, 0, unroll=DMA_ISSUE_UNROLL)
    shared = _swiglu_packed(h2_ref[0], wsg_ref, wsu_ref, wsd_ref)
    for kk in range(TOPK_EXPERTS):
        pltpu.make_async_copy(ys_ref.at[pl.ds(0, tsc)], buf.at[kk], sem).wait()

    half = shared.shape[1] // 2
    gw = gw_ref[0]
    y_lo, y_hi = shared[:, :half], shared[:, half:]
    for kk in range(TOPK_EXPERTS):
        lo, hi = _unpack_bf16_pair(buf[kk])
        w = gw[:, kk:kk + 1]
        y_lo = y_lo + w * lo
        y_hi = y_hi + w * hi
    ms = (jnp.sum(y_lo * y_lo, axis=-1, keepdims=True) + jnp.sum(y_hi * y_hi, axis=-1, keepdims=True)) * (0.5 / half)
    inv = lax.rsqrt(ms + NORM_EPS)
    gate = mod_ref[0, 5:6, :]
    g = g_ref[...]
    o_ref[0, :, :half] = x1_ref[0, :, :half] + gate[:, :half] * (y_lo * inv * g[:, :half])
    o_ref[0, :, half:] = x1_ref[0, :, half:] + gate[:, half:] * (y_hi * inv * g[:, half:])


def _combine(pos, gw_t, x1, h2p, mod, g_post, ws_gate, ws_up, ws_down, ys):
    bsz, seq, d = x1.shape
    dw = h2p.shape[2]
    f = ws_gate.shape[1]
    tsc = min(COMBINE_TILE, seq)
    row = lambda width: pl.BlockSpec((1, tsc, width), lambda b, s: (b, s, 0))
    const = lambda shape: pl.BlockSpec(shape, lambda b, s: tuple(0 for _ in shape))
    return pl.pallas_call(
        _combine_kernel,
        out_shape=jax.ShapeDtypeStruct((bsz, seq, d), F32),
        grid=(bsz, seq // tsc),
        in_specs=[pl.BlockSpec((1, TOPK_EXPERTS, tsc), lambda b, s: (b, 0, s), memory_space=pltpu.SMEM),
                  row(TOPK_EXPERTS), row(d), row(dw),
                  pl.BlockSpec((1, 6, d), lambda b, s: (b, 0, 0)),
                  const((1, d)), const((d, f)), const((d, f)), const((f, d)),
                  pl.BlockSpec(memory_space=pl.ANY)],
        out_specs=row(d),
        scratch_shapes=[pltpu.VMEM((TOPK_EXPERTS, tsc, dw), U32), pltpu.SemaphoreType.DMA],
        compiler_params=_params(("arbitrary", "arbitrary")),
        name="combine",
    )(pos, gw_t, x1, h2p, mod, g_post.reshape(1, d), ws_gate.astype(BF16), ws_up.astype(BF16),
      ws_down.astype(BF16), ys)


def _layer(x, c, w_ada, b_ada, g_mix_pre, g_mix_post, g_ffn_pre, g_ffn_post, w_in, conv_w, conv_b, b_igate, b_fgate,
           g_mlstm_head, g_idx_k, b_idx_k, w_out, w_router, b_router, w_exp_gate, w_exp_up, w_exp_down,
           w_sh_gate, w_sh_up, w_sh_down):
    bsz, seq, d = x.shape
    mod = _adaln(c, w_ada, b_ada).reshape(bsz, 6, d)
    qm, km, vm, om, aq, ak, avt, iq, ik, small = _inproj(x, mod, g_mix_pre, w_in, conv_w, conv_b, b_igate, b_fgate,
                                                        g_idx_k, b_idx_k)
    hm = _mlstm(qm, km, vm, om, small, g_mlstm_head)
    ha = _dsa(iq, ik, small, aq, ak, avt)
    x1, h2p, lt = _outproj(hm, ha, x, mod, g_mix_post, g_ffn_pre, w_out, w_router)

    eid, rnk, gw, cnt = _route(lt, b_router)
    counts = cnt[:, 0].astype(I32)
    padded = (counts + EXPERT_ROWS - 1) // EXPERT_ROWS * EXPERT_ROWS
    pad_end = jnp.cumsum(padded)
    pad_start = pad_end - padded
    pos = rnk
    for e in range(N_EXPERTS):
        pos = pos + jnp.where(eid == e, pad_start[e], 0)
    nb = (bsz * seq * TOPK_EXPERTS + EXPERT_ROWS - 1) // EXPERT_ROWS + N_EXPERTS
    block_start = jnp.arange(nb, dtype=I32) * EXPERT_ROWS
    block_e = jnp.minimum(jnp.sum((pad_end[None, :] <= block_start[:, None]).astype(I32), axis=1), N_EXPERTS - 1)

    xs = _dispatch(pos, h2p, nb * EXPERT_ROWS)
    ys = _experts(block_e, xs, w_exp_gate, w_exp_up, w_exp_down)
    return _combine(pos, gw.transpose(0, 2, 1), x1, h2p, mod, g_ffn_post, w_sh_gate, w_sh_up, w_sh_down, ys)


def kernel(x, c, w_ada, b_ada, g_mix_pre, g_mix_post, g_ffn_pre, g_ffn_post, w_in, conv_w, conv_b, b_igate, b_fgate, g_mlstm_head, g_idx_k, b_idx_k, w_out, w_router, b_router, w_exp_gate, w_exp_up, w_exp_down, w_sh_gate, w_sh_up, w_sh_down):
    per_layer = (w_ada, b_ada, g_mix_pre, g_mix_post, g_ffn_pre, g_ffn_post, w_in, conv_w, conv_b, b_igate, b_fgate,
                 g_mlstm_head, g_idx_k, b_idx_k, w_out, w_router, b_router, w_exp_gate, w_exp_up, w_exp_down,
                 w_sh_gate, w_sh_up, w_sh_down)
    for layer in range(w_ada.shape[0]):
        x = _layer(x, c, *(p[layer] for p in per_layer))
    return x
```
